```python
import math
import jax
import jax.numpy as jnp
from jax import lax
import numpy as np

D_MODEL = 1024
BATCH = 2
SEQ = 8192
DEPTH = 2

CHUNK = 64
EPS = 1e-6
CONV_K = 4
F32 = jnp.float32

S5_WIDTH = 512
S5_GROUP = 16
S5_GROUPS = S5_WIDTH // S5_GROUP
S5_STATE = 64

ML_WIDTH = 512
ML_HEADS = 4
ML_HEAD_DIM = ML_WIDTH // ML_HEADS

M2_WIDTH = 1024
M2_HEAD_DIM = 64
M2_HEADS = M2_WIDTH // M2_HEAD_DIM
M2_GROUPS = 2
M2_HPG = M2_HEADS // M2_GROUPS
M2_STATE = 128
M2_CONV_DIM = M2_WIDTH + 2 * M2_GROUPS * M2_STATE

N_BRANCH = 3
FFN_HIDDEN = -(-(8 * D_MODEL) // (3 * 256)) * 256

IN_SPLITS = (S5_WIDTH, ML_WIDTH, ML_WIDTH, ML_WIDTH, ML_WIDTH, ML_HEADS, ML_HEADS, M2_WIDTH, M2_CONV_DIM, M2_HEADS, N_BRANCH * D_MODEL)
IN_WIDTH = sum(IN_SPLITS)

kernel_name = 'hybrid_s5_mlstm_ssd_block'


def rms_norm(x, w):
    x32 = x.astype(F32)
    y = x32 * lax.rsqrt(jnp.mean(x32 * x32, axis=-1, keepdims=True) + EPS)
    return (y * w.astype(F32)).astype(x.dtype)


def causal_dwconv(x, w, b):
    y = lax.conv_general_dilated(x, w[:, None, :].astype(x.dtype), window_strides=(1,), padding=[(CONV_K - 1, 0)], dimension_numbers=('NWC', 'WIO', 'NWC'), feature_group_count=x.shape[-1])
    return y + b.astype(x.dtype)


def causal_mask():
    return jnp.tril(jnp.ones((CHUNK, CHUNK), dtype=bool))


def _complex_affine_combine(e1, e2):
    a1r, a1i, b1r, b1i = e1
    a2r, a2i, b2r, b2i = e2
    return (a2r * a1r - a2i * a1i, a2r * a1i + a2i * a1r, a2r * b1r - a2i * b1i + b2r, a2r * b1i + a2i * b1r + b2i)


def s5_mixer(u, a_re, a_im, b_re, b_im, c_re, c_im, d_skip, log_dt, w_glu):
    bsz, seq, _ = u.shape
    u32 = u.astype(F32).reshape(bsz, seq, S5_GROUPS, S5_GROUP)
    dt = jnp.exp(log_dt.astype(F32))[:, None]
    lr = a_re.astype(F32)
    li = a_im.astype(F32)
    mag = jnp.exp(lr * dt)
    lb_re = mag * jnp.cos(li * dt)
    lb_im = mag * jnp.sin(li * dt)
    den = lr * lr + li * li
    f_re = ((lb_re - 1.0) * lr + lb_im * li) / den
    f_im = (lb_im * lr - (lb_re - 1.0) * li) / den
    br = b_re.astype(F32)
    bi = b_im.astype(F32)
    bb_re = f_re[..., None] * br - f_im[..., None] * bi
    bb_im = f_re[..., None] * bi + f_im[..., None] * br
    bu_re = jnp.einsum('blgc,gpc->blgp', u32, bb_re)
    bu_im = jnp.einsum('blgc,gpc->blgp', u32, bb_im)
    a_r = jnp.broadcast_to(lb_re, bu_re.shape)
    a_i = jnp.broadcast_to(lb_im, bu_im.shape)
    _, _, s_re, s_im = lax.associative_scan(_complex_affine_combine, (a_r, a_i, bu_re, bu_im), axis=1)
    y = (jnp.einsum('gcp,blgp->blgc', c_re.astype(F32), s_re) - jnp.einsum('gcp,blgp->blgc', c_im.astype(F32), s_im) + d_skip.astype(F32) * u32)
    y = jax.nn.gelu(y.reshape(bsz, seq, S5_WIDTH))
    y = y * jax.nn.sigmoid(y @ w_glu.astype(F32))
    return y.astype(u.dtype)


def mlstm_mixer(q, k, v, o_raw, i_raw, f_raw, conv_w, conv_b, bias_i, bias_f):
    bsz, seq, _ = q.shape
    nc = seq // CHUNK
    qk = jax.nn.silu(causal_dwconv(jnp.concatenate([q, k], axis=-1), conv_w, conv_b)).astype(F32)
    shp = (bsz, nc, CHUNK, ML_HEADS, ML_HEAD_DIM)
    qc = qk[..., :ML_WIDTH].reshape(shp)
    kc = qk[..., ML_WIDTH:].reshape(shp) * (ML_HEAD_DIM ** -0.5)
    vc = v.astype(F32).reshape(shp)
    gshp = (bsz, nc, CHUNK, ML_HEADS)
    ig = (i_raw.astype(F32) + bias_i.astype(F32)).reshape(gshp)
    lf = jax.nn.log_sigmoid(f_raw.astype(F32) + bias_f.astype(F32)).reshape(gshp)
    b = jnp.cumsum(lf, axis=2)
    b_end = b[:, :, -1]
    a_end = b_end[:, :, None] - b + ig
    m_loc = jnp.max(a_end, axis=2)
    w_end = jnp.exp(a_end - m_loc[:, :, None])
    c_loc = jnp.einsum('bclh,bclhv,bclhk->bchvk', w_end, vc, kc)
    n_loc = jnp.einsum('bclh,bclhk->bchk', w_end, kc)

    def step(carry, inp):
        c, n, m = carry
        be, ml, cl, nl = inp
        m_new = jnp.maximum(be + m, ml)
        sp = jnp.exp(be + m - m_new)
        sl = jnp.exp(ml - m_new)
        c_new = sp[..., None, None] * c + sl[..., None, None] * cl
        n_new = sp[..., None] * n + sl[..., None] * nl
        return (c_new, n_new, m_new), (c, n, m)

    init = (jnp.zeros((bsz, ML_HEADS, ML_HEAD_DIM, ML_HEAD_DIM), F32), jnp.zeros((bsz, ML_HEADS, ML_HEAD_DIM), F32), jnp.zeros((bsz, ML_HEADS), F32))
    xs_in = (jnp.moveaxis(b_end, 1, 0), jnp.moveaxis(m_loc, 1, 0), jnp.moveaxis(c_loc, 1, 0), jnp.moveaxis(n_loc, 1, 0))
    _, (c_prev, n_prev, m_prev) = lax.scan(step, init, xs_in)
    c_prev = jnp.moveaxis(c_prev, 0, 1)
    n_prev = jnp.moveaxis(n_prev, 0, 1)
    m_prev = jnp.moveaxis(m_prev, 0, 1)
    b_t = jnp.moveaxis(b, 2, -1)
    ig_t = jnp.moveaxis(ig, 2, -1)
    dmat = jnp.where(causal_mask(), b_t[..., :, None] - b_t[..., None, :] + ig_t[..., None, :], -jnp.inf)
    inter = b_t + m_prev[..., None]
    m_t = jnp.maximum(jnp.max(dmat, axis=-1), inter)
    s = jnp.einsum('bcthd,bcshd->bchts', qc, kc) * jnp.exp(dmat - m_t[..., None])
    w_inter = jnp.exp(inter - m_t)
    num = (jnp.einsum('bchts,bcshd->bcthd', s, vc) + jnp.moveaxis(w_inter, -1, 2)[..., None] * jnp.einsum('bchvk,bcthk->bcthv', c_prev, qc))
    den = jnp.sum(s, axis=-1) + w_inter * jnp.einsum('bchk,bcthk->bcht', n_prev, qc)
    denom = jnp.maximum(jnp.abs(den), jnp.exp(-m_t))
    h = num / jnp.moveaxis(denom, -1, 2)[..., None]
    h = h.reshape(bsz, seq, ML_WIDTH) * jax.nn.sigmoid(o_raw.astype(F32))
    return h.astype(q.dtype)


def mamba2_mixer(z, xbc, dt_raw, conv_w, conv_b, dt_bias, a_log, d_skip, norm_w):
    bsz, seq, _ = z.shape
    nc = seq // CHUNK
    gn = M2_GROUPS * M2_STATE
    xbc = jax.nn.silu(causal_dwconv(xbc, conv_w, conv_b)).astype(F32)
    xs = xbc[..., :M2_WIDTH].reshape(bsz, nc, CHUNK, M2_GROUPS, M2_HPG, M2_HEAD_DIM)
    bm = xbc[..., M2_WIDTH:M2_WIDTH + gn].reshape(bsz, nc, CHUNK, M2_GROUPS, M2_STATE)
    cm = xbc[..., M2_WIDTH + gn:].reshape(bsz, nc, CHUNK, M2_GROUPS, M2_STATE)
    dt = jax.nn.softplus(dt_raw.astype(F32) + dt_bias.astype(F32)).reshape(bsz, nc, CHUNK, M2_GROUPS, M2_HPG)
    a = -jnp.exp(a_log.astype(F32)).reshape(M2_GROUPS, M2_HPG)
    cum = jnp.cumsum(dt * a, axis=2)
    cum_t = jnp.moveaxis(cum, 2, -1)
    dt_t = jnp.moveaxis(dt, 2, -1)
    seg = cum_t[..., :, None] - cum_t[..., None, :]
    decay = jnp.exp(jnp.where(causal_mask(), seg, -jnp.inf))
    cb = jnp.einsum('bctgn,bcsgn->bcgts', cm, bm)
    w_ts = cb[:, :, :, None] * decay * dt_t[..., None, :]
    y_diag = jnp.einsum('bcgrts,bcsgrp->bctgrp', w_ts, xs)
    w_end = jnp.exp(cum_t[..., -1:] - cum_t) * dt_t
    s_loc = jnp.einsum('bcgrs,bcsgn,bcsgrp->bcgrpn', w_end, bm, xs)
    chunk_decay = jnp.exp(cum_t[..., -1])

    def step(s, inp):
        dec, sl = inp
        return dec[..., None, None] * s + sl, s

    s0 = jnp.zeros((bsz, M2_GROUPS, M2_HPG, M2_HEAD_DIM, M2_STATE), F32)
    _, s_prev = lax.scan(step, s0, (jnp.moveaxis(chunk_decay, 1, 0), jnp.moveaxis(s_loc, 1, 0)))
    s_prev = jnp.moveaxis(s_prev, 0, 1)
    y_off = jnp.einsum('bctgn,bcgrpn->bctgrp', cm, s_prev) * jnp.exp(cum)[..., None]
    y = y_diag + y_off + d_skip.astype(F32).reshape(M2_GROUPS, M2_HPG)[..., None] * xs
    y = y.reshape(bsz, seq, M2_WIDTH) * jax.nn.silu(z.astype(F32))
    yg = y.reshape(bsz, seq, M2_GROUPS, M2_WIDTH // M2_GROUPS)
    yg = yg * lax.rsqrt(jnp.mean(yg * yg, axis=-1, keepdims=True) + EPS)
    return (yg.reshape(bsz, seq, M2_WIDTH) * norm_w.astype(F32)).astype(z.dtype)


def hybrid_layer(x, norm_mix_pre, norm_mix_post, w_in, s5_a_re, s5_a_im, s5_b_re, s5_b_im, s5_c_re, s5_c_im, s5_d, s5_log_dt, s5_w_glu, ml_conv_w, ml_conv_b, ml_bias_i, ml_bias_f, m2_conv_w, m2_conv_b, m2_dt_bias, m2_a_log, m2_d, m2_norm_w, w_br_a, w_br_b, w_br_c, w_out, norm_ffn_pre, norm_ffn_post, w_ffn_gate, w_ffn_up, w_ffn_down):
    bsz, seq, _ = x.shape
    h = rms_norm(x, norm_mix_pre)
    proj = h @ w_in
    offsets = np.cumsum(IN_SPLITS)[:-1].tolist()
    u_s5, q, k, v, o_raw, i_raw, f_raw, z, xbc, dt_raw, gate_raw = jnp.split(proj, offsets, axis=-1)
    y_a = s5_mixer(u_s5, s5_a_re, s5_a_im, s5_b_re, s5_b_im, s5_c_re, s5_c_im, s5_d, s5_log_dt, s5_w_glu)
    y_b = mlstm_mixer(q, k, v, o_raw, i_raw, f_raw, ml_conv_w, ml_conv_b, ml_bias_i, ml_bias_f)
    y_c = mamba2_mixer(z, xbc, dt_raw, m2_conv_w, m2_conv_b, m2_dt_bias, m2_a_log, m2_d, m2_norm_w)
    gates = jax.nn.sigmoid(gate_raw.astype(F32)).reshape(bsz, seq, N_BRANCH, D_MODEL)
    merged = (gates[:, :, 0] * (y_a @ w_br_a).astype(F32) + gates[:, :, 1] * (y_b @ w_br_b).astype(F32) + gates[:, :, 2] * (y_c @ w_br_c).astype(F32))
    mix = merged.astype(x.dtype) @ w_out
    x = x + rms_norm(mix, norm_mix_post)
    h = rms_norm(x, norm_ffn_pre)
    ffn = (jax.nn.silu(h @ w_ffn_gate) * (h @ w_ffn_up)) @ w_ffn_down
    return x + rms_norm(ffn, norm_ffn_post)


def setup_inputs(seed: int = 0) -> dict:
    key = jax.random.key(seed)
    ks = jax.random.split(key, 40)

    def nrm(i, shape, scale):
        return jax.random.normal(ks[i], shape, F32) * scale

    def gain(i):
        return 1.0 + nrm(i, (DEPTH, D_MODEL), 0.01)

    G, P, C = S5_GROUPS, S5_STATE, S5_GROUP
    dt0 = jnp.exp(jax.random.uniform(ks[22], (DEPTH, M2_HEADS), F32, math.log(1e-3), math.log(1e-1)))
    return {
        'x': nrm(0, (BATCH, SEQ, D_MODEL), 1.0),
        'norm_mix_pre': gain(1),
        'norm_mix_post': gain(2),
        'w_in': nrm(3, (DEPTH, D_MODEL, IN_WIDTH), D_MODEL ** -0.5),
        's5_a_re': -0.5 + nrm(4, (DEPTH, G, P), 0.01),
        's5_a_im': math.pi * jnp.arange(P, dtype=F32) + nrm(5, (DEPTH, G, P), 0.01),
        's5_b_re': nrm(6, (DEPTH, G, P, C), (2 * C) ** -0.5),
        's5_b_im': nrm(7, (DEPTH, G, P, C), (2 * C) ** -0.5),
        's5_c_re': nrm(8, (DEPTH, G, C, P), P ** -0.5),
        's5_c_im': nrm(9, (DEPTH, G, C, P), P ** -0.5),
        's5_d': nrm(10, (DEPTH, G, C), 1.0),
        's5_log_dt': jax.random.uniform(ks[11], (DEPTH, G), F32, math.log(1e-3), math.log(1e-1)),
        's5_w_glu': nrm(12, (DEPTH, S5_WIDTH, S5_WIDTH), S5_WIDTH ** -0.5),
        'ml_conv_w': nrm(13, (DEPTH, CONV_K, 2 * ML_WIDTH), CONV_K ** -0.5),
        'ml_conv_b': nrm(14, (DEPTH, 2 * ML_WIDTH), 0.01),
        'ml_bias_i': nrm(15, (DEPTH, ML_HEADS), 0.1),
        'ml_bias_f': jnp.linspace(3.0, 6.0, ML_HEADS, dtype=F32) + nrm(16, (DEPTH, ML_HEADS), 0.01),
        'm2_conv_w': nrm(17, (DEPTH, CONV_K, M2_CONV_DIM), CONV_K ** -0.5),
        'm2_conv_b': nrm(18, (DEPTH, M2_CONV_DIM), 0.01),
        'm2_dt_bias': dt0 + jnp.log(-jnp.expm1(-dt0)),
        'm2_a_log': jnp.log(jax.random.uniform(ks[19], (DEPTH, M2_HEADS), F32, 1.0, 16.0)),
        'm2_d': 1.0 + nrm(20, (DEPTH, M2_HEADS), 0.01),
        'm2_norm_w': 1.0 + nrm(21, (DEPTH, M2_WIDTH), 0.01),
        'w_br_a': nrm(23, (DEPTH, S5_WIDTH, D_MODEL), S5_WIDTH ** -0.5),
        'w_br_b': nrm(24, (DEPTH, ML_WIDTH, D_MODEL), ML_WIDTH ** -0.5),
        'w_br_c': nrm(25, (DEPTH, M2_WIDTH, D_MODEL), M2_WIDTH ** -0.5),
        'w_out': nrm(26, (DEPTH, D_MODEL, D_MODEL), D_MODEL ** -0.5),
        'norm_ffn_pre': gain(27),
        'norm_ffn_post': gain(28),
        'w_ffn_gate': nrm(29, (DEPTH, D_MODEL, FFN_HIDDEN), D_MODEL ** -0.5),
        'w_ffn_up': nrm(30, (DEPTH, D_MODEL, FFN_HIDDEN), D_MODEL ** -0.5),
        'w_ffn_down': nrm(31, (DEPTH, FFN_HIDDEN, D_MODEL), FFN_HIDDEN ** -0.5),
    }


def reference(x, norm_mix_pre, norm_mix_post, w_in, s5_a_re, s5_a_im, s5_b_re, s5_b_im, s5_c_re, s5_c_im, s5_d, s5_log_dt, s5_w_glu, ml_conv_w, ml_conv_b, ml_bias_i, ml_bias_f, m2_conv_w, m2_conv_b, m2_dt_bias, m2_a_log, m2_d, m2_norm_w, w_br_a, w_br_b, w_br_c, w_out, norm_ffn_pre, norm_ffn_post, w_ffn_gate, w_ffn_up, w_ffn_down):
    for l in range(DEPTH):
        x = hybrid_layer(x, norm_mix_pre[l], norm_mix_post[l], w_in[l], s5_a_re[l], s5_a_im[l], s5_b_re[l], s5_b_im[l], s5_c_re[l], s5_c_im[l], s5_d[l], s5_log_dt[l], s5_w_glu[l], ml_conv_w[l], ml_conv_b[l], ml_bias_i[l], ml_bias_f[l], m2_conv_w[l], m2_conv_b[l], m2_dt_bias[l], m2_a_log[l], m2_d[l], m2_norm_w[l], w_br_a[l], w_br_b[l], w_br_c[l], w_out[l], norm_ffn_pre[l], norm_ffn_post[l], w_ffn_gate[l], w_ffn_up[l], w_ffn_down[l])
    return x
```

```python
import functools
import math

import jax
import jax.numpy as jnp
import numpy as np
from jax import lax
from jax.experimental import pallas as pl
from jax.experimental.pallas import tpu as pltpu

F32 = jnp.float32
BF16 = jnp.bfloat16

D_MODEL = 1024
EPS = 1e-6
CONV_K = 4

S5_WIDTH = 512
S5_GROUP = 16
S5_GROUPS = S5_WIDTH // S5_GROUP
S5_STATE = 64
S5_CHUNK = 16
S5_K = S5_CHUNK * S5_GROUP

ML_WIDTH = 512
ML_HEADS = 4
ML_HEAD_DIM = ML_WIDTH // ML_HEADS

M2_WIDTH = 1024
M2_HEAD_DIM = 64
M2_HEADS = M2_WIDTH // M2_HEAD_DIM
M2_GROUPS = 2
M2_HPG = M2_HEADS // M2_GROUPS
M2_STATE = 128
M2_CONV_DIM = M2_WIDTH + 2 * M2_GROUPS * M2_STATE
M2_GROUP_WIDTH = M2_WIDTH // M2_GROUPS

N_BRANCH = 3
FFN_HIDDEN = -(-(8 * D_MODEL) // (3 * 256)) * 256

LANES = 128
TAIL = 8

SLOT_I = 0
SLOT_F = ML_HEADS
SLOT_DT = 2 * ML_HEADS

SEQ_CHUNK = 128
TOKEN_BLOCK = 256

VMEM_LIMIT = 56 * 1024 * 1024


def _cparams(*sem):
    return pltpu.CompilerParams(dimension_semantics=sem, vmem_limit_bytes=VMEM_LIMIT)


def _const_spec(shape):
    nd = len(shape)
    return pl.BlockSpec(shape, lambda *_: (0,) * nd, pipeline_mode=pl.Buffered(1))


def _rms(x, w):
    return x * lax.rsqrt(jnp.mean(x * x, axis=-1, keepdims=True) + EPS) * w


def _split3(x):
    hi = x.astype(BF16)
    r1 = x - hi.astype(F32)
    mid = r1.astype(BF16)
    lo = (r1 - mid.astype(F32)).astype(BF16)
    return hi, mid, lo


def _dot(a, b):
    return jnp.dot(a, b, preferred_element_type=F32)


def _dot3_left(m_bf16, x):
    hi, mid, lo = _split3(x)
    return _dot(m_bf16, hi) + _dot(m_bf16, mid) + _dot(m_bf16, lo)


def _dot3_right(x, m_bf16):
    hi, mid, lo = _split3(x)
    return _dot(hi, m_bf16) + _dot(mid, m_bf16) + _dot(lo, m_bf16)


def _lower_tri(n):
    r = lax.broadcasted_iota(jnp.int32, (n, n), 0)
    c = lax.broadcasted_iota(jnp.int32, (n, n), 1)
    return r >= c


IN_SEGMENTS = (
    ("u", S5_WIDTH, BF16),
    ("qk", 2 * ML_WIDTH, BF16),
    ("v", ML_WIDTH, BF16),
    ("o", ML_WIDTH, BF16),
    ("z", M2_WIDTH, BF16),
    ("xbc", M2_CONV_DIM, BF16),
    ("gates", N_BRANCH * D_MODEL, BF16),
    ("small", LANES, F32),
)
IN_PACKED = sum(w for _, w, _ in IN_SEGMENTS)


def _pack_w_in(w_in):
    splits = (S5_WIDTH, ML_WIDTH, ML_WIDTH, ML_WIDTH, ML_WIDTH, ML_HEADS, ML_HEADS, M2_WIDTH, M2_CONV_DIM, M2_HEADS, N_BRANCH * D_MODEL)
    offs = np.cumsum((0,) + splits)
    u, q, k, v, o, i, f, z, xbc, dt, g = (w_in[:, offs[n]:offs[n + 1]] for n in range(len(splits)))
    pad = jnp.zeros((w_in.shape[0], LANES - 2 * ML_HEADS - M2_HEADS), w_in.dtype)
    return jnp.concatenate([u, q, k, v, o, z, xbc, g, i, f, dt, pad], axis=1).astype(BF16)


def _in_proj_kernel(x_ref, nw_ref, w_ref, *out_refs):
    hb = _rms(x_ref[...], nw_ref[...]).astype(BF16)
    off = 0
    for ref, (_, width, _) in zip(out_refs, IN_SEGMENTS):
        ref[...] = _dot(hb, w_ref[:, off:off + width]).astype(ref.dtype)
        off += width


def _in_proj(x2, norm_w, w_packed):
    t = x2.shape[0]
    tm = min(TOKEN_BLOCK, t)
    return pl.pallas_call(
        _in_proj_kernel,
        grid=(t // tm,),
        in_specs=[
            pl.BlockSpec((tm, D_MODEL), lambda i: (i, 0)),
            _const_spec((1, D_MODEL)),
            _const_spec((D_MODEL, IN_PACKED)),
        ],
        out_specs=[pl.BlockSpec((tm, w), lambda i: (i, 0)) for _, w, _ in IN_SEGMENTS],
        out_shape=[jax.ShapeDtypeStruct((t, w), dt) for _, w, dt in IN_SEGMENTS],
        compiler_params=_cparams("parallel"),
        name="in_proj",
    )(x2, norm_w.reshape(1, D_MODEL), w_packed)


def _s5_prep_kernel(lr_row_ref, li_row_ref, lr_col_ref, li_col_ref, logdt_ref, btr_ref, bti_ref, ctr_ref, cti_ref, d_ref,
                    t_ref, w_ref, m_ref, pq_ref):
    hi = lax.Precision.HIGHEST
    dt = jnp.exp(logdt_ref[0])

    def discretise(lr, li):
        mag = jnp.exp(lr * dt)
        lb_re = mag * jnp.cos(li * dt)
        lb_im = mag * jnp.sin(li * dt)
        den = lr * lr + li * li
        f_re = ((lb_re - 1.0) * lr + lb_im * li) / den
        f_im = (lb_im * lr - (lb_re - 1.0) * li) / den
        return lb_re, lb_im, f_re, f_im

    def power(lr, li, k):
        mag = jnp.exp(lr * dt * k)
        ang = li * dt * k
        return mag * jnp.cos(ang), mag * jnp.sin(ang)

    lr_row, li_row = lr_row_ref[0], li_row_ref[0]
    lr_col, li_col = lr_col_ref[0], li_col_ref[0]
    _, _, f_re, f_im = discretise(lr_row, li_row)
    btr, bti = btr_ref[0], bti_ref[0]
    bbr = f_re * btr - f_im * bti
    bbi = f_re * bti + f_im * btr

    s_of_row = (lax.broadcasted_iota(jnp.int32, (S5_K, 1), 0) // S5_GROUP).astype(F32)
    pr, pi = power(lr_row, li_row, (S5_CHUNK - 1.0) - s_of_row)
    w_re = pr * bbr - pi * bbi
    w_im = pr * bbi + pi * bbr
    p64 = S5_STATE
    w_ref[0, :, 0 * p64:1 * p64] = w_re.astype(w_ref.dtype)
    w_ref[0, :, 1 * p64:2 * p64] = w_im.astype(w_ref.dtype)
    w_ref[0, :, 2 * p64:3 * p64] = w_im.astype(w_ref.dtype)
    w_ref[0, :, 3 * p64:4 * p64] = w_re.astype(w_ref.dtype)

    t_of_lane_i = lax.broadcasted_iota(jnp.int32, (1, S5_K), 1) // S5_GROUP
    ar, ai = power(lr_col, li_col, t_of_lane_i.astype(F32))
    ctr, cti = ctr_ref[0], cti_ref[0]
    x0 = ar * ctr - ai * cti
    y0 = ar * cti + ai * ctr
    lb_re_c, lb_im_c, _, _ = discretise(lr_col, li_col)
    x1 = lb_re_c * x0 - lb_im_c * y0
    y1 = lb_re_c * y0 + lb_im_c * x0
    m_ref[0, 0:p64, :] = x1.astype(m_ref.dtype)
    m_ref[0, p64:2 * p64, :] = (-y1).astype(m_ref.dtype)

    bbr16 = bbr[0:S5_GROUP, :]
    bbi16 = bbi[0:S5_GROUP, :]
    rows = lax.broadcasted_iota(jnp.int32, (S5_GROUP, S5_K), 0)
    cols = lax.broadcasted_iota(jnp.int32, (S5_GROUP, S5_K), 1)
    d_row = d_ref[0]
    lane = lax.broadcasted_iota(jnp.int32, (p64, S5_K), 1)
    for s in range(S5_CHUNK):
        if s == 0:
            xs, ys = x0, y0
        else:
            keep = lane >= s * S5_GROUP
            xs = jnp.where(keep, pltpu.roll(x0, s * S5_GROUP, 1), 0.0)
            ys = jnp.where(keep, pltpu.roll(y0, s * S5_GROUP, 1), 0.0)
        blk = jnp.dot(bbr16, xs, precision=hi, preferred_element_type=F32) - jnp.dot(bbi16, ys, precision=hi, preferred_element_type=F32)
        blk = blk + jnp.where(cols == rows + s * S5_GROUP, d_row, 0.0)
        t_ref[0, s * S5_GROUP:(s + 1) * S5_GROUP, :] = blk.astype(t_ref.dtype)

    cr, ci = power(lr_row, li_row, float(S5_CHUNK))
    pq_ref[0, 0:1, :] = jnp.concatenate([cr, cr, cr, cr], axis=1)
    pq_ref[0, 1:2, :] = jnp.concatenate([-ci, ci, ci, -ci], axis=1)


def _s5_prep(a_re, a_im, b_re, b_im, c_re, c_im, d_skip, log_dt):
    g, p, c = S5_GROUPS, S5_STATE, S5_GROUP
    tile_b = lambda b: jnp.tile(jnp.transpose(b, (0, 2, 1)), (1, S5_CHUNK, 1))
    tile_c = lambda cc: jnp.tile(jnp.transpose(cc, (0, 2, 1)), (1, 1, S5_CHUNK))
    args = (
        a_re.reshape(g, 1, p), a_im.reshape(g, 1, p), a_re.reshape(g, p, 1), a_im.reshape(g, p, 1),
        log_dt.reshape(g, 1, 1), tile_b(b_re), tile_b(b_im), tile_c(c_re), tile_c(c_im),
        jnp.tile(d_skip, (1, S5_CHUNK)).reshape(g, 1, S5_K),
    )
    spec = lambda a: pl.BlockSpec((1,) + a.shape[1:], lambda i: (i, 0, 0))
    out_shapes = (
        jax.ShapeDtypeStruct((g, S5_K, S5_K), BF16),
        jax.ShapeDtypeStruct((g, S5_K, 4 * p), BF16),
        jax.ShapeDtypeStruct((g, 2 * p, S5_K), BF16),
        jax.ShapeDtypeStruct((g, 2, 4 * p), F32),
    )
    return pl.pallas_call(
        _s5_prep_kernel,
        grid=(g,),
        in_specs=[spec(a) for a in args],
        out_specs=[spec(o) for o in out_shapes],
        out_shape=out_shapes,
        compiler_params=_cparams("parallel"),
        name="s5_prep",
    )(*args)


def _s5_local_kernel(u_ref, w_ref, l_ref):
    l_ref[0] = _dot(u_ref[0], w_ref[0])


def _s5_scan_kernel(l_ref, pq_ref, s_ref, state_scr):
    @pl.when(pl.program_id(1) == 0)
    def _():
        state_scr[...] = jnp.zeros_like(state_scr)

    p_mul = pq_ref[:, 0, :]
    q_mul = pq_ref[:, 1, :]
    half = 2 * S5_STATE

    def step(j, x):
        s_ref[j] = x[:, :half]
        swapped = jnp.concatenate([x[:, half:], x[:, :half]], axis=1)
        return p_mul * x + q_mul * swapped + l_ref[j]

    state_scr[...] = lax.fori_loop(0, l_ref.shape[0], step, state_scr[...], unroll=8)


def _s5_out_kernel(u_ref, s_ref, t_ref, m_ref, y_ref):
    y = _dot(u_ref[0], t_ref[0]) + _dot(s_ref[0].astype(BF16), m_ref[0])
    y_ref[0] = jax.nn.gelu(y).astype(y_ref.dtype)


def _s5_mixer(u2, bsz, mats):
    t_mat, w_mat, m_mat, pq = mats
    t = u2.shape[0]
    g = S5_GROUPS
    nch = t // S5_CHUNK
    nch_b = nch // bsz
    ug = u2.reshape(nch, S5_CHUNK, g, S5_GROUP).transpose(2, 0, 1, 3).reshape(g, nch, S5_K)

    gspec = lambda *tail: pl.BlockSpec((1,) + tail, lambda i: (i, 0, 0))
    l_loc = pl.pallas_call(
        _s5_local_kernel,
        grid=(g,),
        in_specs=[gspec(nch, S5_K), gspec(S5_K, 4 * S5_STATE)],
        out_specs=gspec(nch, 4 * S5_STATE),
        out_shape=jax.ShapeDtypeStruct((g, nch, 4 * S5_STATE), F32),
        compiler_params=_cparams("parallel"),
        name="s5_local",
    )(ug, w_mat)

    jb = min(64, nch_b)
    l_t = jnp.transpose(l_loc, (1, 0, 2))
    s_prev_t = pl.pallas_call(
        _s5_scan_kernel,
        grid=(bsz, nch_b // jb),
        in_specs=[
            pl.BlockSpec((jb, g, 4 * S5_STATE), lambda b, j: (b * (nch_b // jb) + j, 0, 0)),
            _const_spec((g, 2, 4 * S5_STATE)),
        ],
        out_specs=pl.BlockSpec((jb, g, 2 * S5_STATE), lambda b, j: (b * (nch_b // jb) + j, 0, 0)),
        out_shape=jax.ShapeDtypeStruct((nch, g, 2 * S5_STATE), F32),
        scratch_shapes=[pltpu.VMEM((g, 4 * S5_STATE), F32)],
        compiler_params=_cparams("arbitrary", "arbitrary"),
        name="s5_scan",
    )(l_t, pq)
    s_prev = jnp.transpose(s_prev_t, (1, 0, 2))

    yg = pl.pallas_call(
        _s5_out_kernel,
        grid=(g,),
        in_specs=[gspec(nch, S5_K), gspec(nch, 2 * S5_STATE), gspec(S5_K, S5_K), gspec(2 * S5_STATE, S5_K)],
        out_specs=gspec(nch, S5_K),
        out_shape=jax.ShapeDtypeStruct((g, nch, S5_K), BF16),
        compiler_params=_cparams("parallel"),
        name="s5_out",
    )(ug, s_prev, t_mat, m_mat)
    return yg.reshape(g, nch, S5_CHUNK, S5_GROUP).transpose(1, 2, 0, 3).reshape(t, S5_WIDTH)


def _causal_conv(x_ref, tail_ref, w_ref, b_ref, pad_scr, first):
    n = x_ref.shape[0]
    tail = tail_ref[...].astype(F32)
    pad_scr[0:TAIL, :] = jnp.where(first, 0.0, tail)
    pad_scr[TAIL:TAIL + n, :] = x_ref[...].astype(F32)
    acc = b_ref[...] + w_ref[CONV_K - 1:CONV_K, :] * pad_scr[TAIL:TAIL + n, :]
    for j in range(CONV_K - 1):
        acc = acc + w_ref[j:j + 1, :] * pad_scr[pl.ds(TAIL - (CONV_K - 1) + j, n), :]
    return acc


def _silu(x):
    return x * jax.nn.sigmoid(x)


def _mlstm_kernel(qk_ref, tail_ref, v_ref, o_ref, small_ref, cw_ref, cb_ref, gbias_ref, y_ref, pad_scr, ct_scr, m_scr):
    first = pl.program_id(1) == 0
    n = qk_ref.shape[0]
    dh = ML_HEAD_DIM

    @pl.when(first)
    def _():
        ct_scr[...] = jnp.zeros_like(ct_scr)
        m_scr[...] = jnp.zeros_like(m_scr)

    qk = _silu(_causal_conv(qk_ref, tail_ref, cw_ref, cb_ref, pad_scr, first))
    gate = small_ref[...] + gbias_ref[...]
    lf = jax.nn.log_sigmoid(gate)
    tri = _lower_tri(n)
    b_all = _dot3_left(tri.astype(BF16), lf)
    gate_t = gate.T
    b_t = b_all.T
    ones_col = (lax.broadcasted_iota(jnp.int32, (n, dh), 1) == 0).astype(BF16)

    for h in range(ML_HEADS):
        q = qk[:, h * dh:(h + 1) * dh].astype(BF16)
        k = qk[:, ML_WIDTH + h * dh:ML_WIDTH + (h + 1) * dh] * (dh ** -0.5)
        k_t = k.T.astype(BF16)
        v_ext = jnp.concatenate([v_ref[:, h * dh:(h + 1) * dh], ones_col], axis=1)
        b_col = b_all[:, SLOT_F + h:SLOT_F + h + 1]
        ig_col = gate[:, SLOT_I + h:SLOT_I + h + 1]
        b_row = b_t[SLOT_F + h:SLOT_F + h + 1, :]
        ig_row = gate_t[SLOT_I + h:SLOT_I + h + 1, :]
        m_prev = m_scr[h][:, 0:1]
        ct_prev = ct_scr[h]

        dmat = jnp.where(tri, b_col - b_row + ig_row, -jnp.inf)
        inter = b_col + m_prev
        m_t = jnp.maximum(jnp.max(dmat, axis=1, keepdims=True), inter)
        p = _dot(q, k_t) * jnp.exp(dmat - m_t)
        w_inter = jnp.exp(inter - m_t)
        num_den = _dot(p.astype(BF16), v_ext) + w_inter * _dot(q, ct_prev.astype(BF16))
        den = num_den[:, dh:dh + 1]
        denom = jnp.maximum(jnp.abs(den), jnp.exp(-m_t))
        out = num_den[:, :dh] / denom * jax.nn.sigmoid(o_ref[:, h * dh:(h + 1) * dh].astype(F32))
        y_ref[:, h * dh:(h + 1) * dh] = out.astype(y_ref.dtype)

        b_end = b_col[n - 1:n, :]
        a_end = b_end - b_col + ig_col
        m_loc = jnp.max(a_end, axis=0, keepdims=True)
        w_end = jnp.exp(a_end - m_loc)
        m_new = jnp.maximum(b_end + m_prev, m_loc)
        s_prev = jnp.exp(b_end + m_prev - m_new)
        s_loc = jnp.exp(m_loc - m_new)
        upd = _dot(k_t, (w_end * v_ext.astype(F32)).astype(BF16))
        ct_scr[h] = s_prev * ct_prev + s_loc * upd
        m_scr[h] = jnp.broadcast_to(m_new, m_scr.shape[1:])


def _mlstm(qk, v, o, small, conv_w, conv_b, gbias, bsz):
    t = qk.shape[0]
    seq = t // bsz
    n = min(SEQ_CHUNK, seq)
    nc = seq // n
    row = lambda w: pl.BlockSpec((n, w), lambda b, c: (b * nc + c, 0))
    tail_blocks = n // TAIL
    return pl.pallas_call(
        _mlstm_kernel,
        grid=(bsz, nc),
        in_specs=[
            row(2 * ML_WIDTH),
            pl.BlockSpec((TAIL, 2 * ML_WIDTH), lambda b, c: (jnp.maximum((b * nc + c) * tail_blocks - 1, 0), 0)),
            row(ML_WIDTH), row(ML_WIDTH), row(LANES),
            _const_spec((CONV_K, 2 * ML_WIDTH)), _const_spec((1, 2 * ML_WIDTH)), _const_spec((1, LANES)),
        ],
        out_specs=row(ML_WIDTH),
        out_shape=jax.ShapeDtypeStruct((t, ML_WIDTH), BF16),
        scratch_shapes=[
            pltpu.VMEM((TAIL + n, 2 * ML_WIDTH), F32),
            pltpu.VMEM((ML_HEADS, ML_HEAD_DIM, 2 * ML_HEAD_DIM), F32),
            pltpu.VMEM((ML_HEADS, 1, LANES), F32),
        ],
        compiler_params=_cparams("arbitrary", "arbitrary"),
        name="mlstm",
    )(qk, qk, v, o, small, conv_w, conv_b.reshape(1, -1), gbias)


def _ssd_kernel(z_ref, xbc_ref, tail_ref, small_ref, cw_ref, cb_ref, gbias_ref, alog_ref, dskip_ref, normw_ref, expand_ref,
                y_ref, pad_scr, state_scr, ydiag_scr):
    first = pl.program_id(1) == 0
    n = z_ref.shape[0]
    gw = M2_GROUP_WIDTH
    hd = M2_HEAD_DIM

    @pl.when(first)
    def _():
        state_scr[...] = jnp.zeros_like(state_scr)

    xbc = _silu(_causal_conv(xbc_ref, tail_ref, cw_ref, cb_ref, pad_scr, first))
    xs = xbc[:, :M2_WIDTH]

    lane = lax.broadcasted_iota(jnp.int32, (1, LANES), 1)
    dt_lanes = (lane >= SLOT_DT) & (lane < SLOT_DT + M2_HEADS)
    dt = jnp.where(dt_lanes, jax.nn.softplus(small_ref[...] + gbias_ref[...]), 0.0)
    a_row = jnp.where(dt_lanes, -jnp.exp(alog_ref[...]), 0.0)
    tri = _lower_tri(n)
    cum = _dot3_left(tri.astype(BF16), dt * a_row)
    cum_t = cum.T
    expand = expand_ref[...]
    dt_e = _dot3_right(dt, expand)
    cum_e = _dot3_right(cum, expand)
    cum_end_e = cum_e[n - 1:n, :]
    xdt = xs * dt_e
    x_in = (xdt * jnp.exp(cum_end_e - cum_e)).astype(BF16)
    decay_in = jnp.exp(cum_e)
    decay_chunk = jnp.exp(cum_end_e)
    xdt_b = xdt.astype(BF16)

    for g in range(M2_GROUPS):
        b_g = xbc[:, M2_WIDTH + g * M2_STATE:M2_WIDTH + (g + 1) * M2_STATE]
        c_g = xbc[:, M2_WIDTH + (M2_GROUPS + g) * M2_STATE:M2_WIDTH + (M2_GROUPS + g + 1) * M2_STATE].astype(BF16)
        b_t = b_g.T.astype(BF16)
        cb = _dot(c_g, b_t)
        for r in range(M2_HPG):
            head = g * M2_HPG + r
            cum_col = cum[:, SLOT_DT + head:SLOT_DT + head + 1]
            cum_row = cum_t[SLOT_DT + head:SLOT_DT + head + 1, :]
            w = cb * jnp.exp(jnp.where(tri, cum_col - cum_row, -jnp.inf))
            ydiag_scr[:, head * hd:(head + 1) * hd] = _dot(w.astype(BF16), xdt_b[:, head * hd:(head + 1) * hd])
        cols = slice(g * gw, (g + 1) * gw)
        st = state_scr[g]
        y_off = _dot(c_g, st.astype(BF16)) * decay_in[:, cols]
        y = ydiag_scr[:, cols] + y_off + dskip_ref[:, cols] * xs[:, cols]
        y = y * _silu(z_ref[:, cols].astype(F32))
        y = y * lax.rsqrt(jnp.mean(y * y, axis=-1, keepdims=True) + EPS)
        y_ref[:, cols] = (y * normw_ref[:, cols]).astype(y_ref.dtype)
        state_scr[g] = st * decay_chunk[:, cols] + _dot(b_t, x_in[:, cols])


def _ssd(z, xbc, small, conv_w, conv_b, gbias, alog_row, dskip_row, norm_w, bsz):
    t = z.shape[0]
    seq = t // bsz
    n = min(SEQ_CHUNK, seq)
    nc = seq // n
    row = lambda w: pl.BlockSpec((n, w), lambda b, c: (b * nc + c, 0))
    tail_blocks = n // TAIL
    expand = np.zeros((LANES, M2_WIDTH), np.float32)
    for h in range(M2_HEADS):
        expand[SLOT_DT + h, h * M2_HEAD_DIM:(h + 1) * M2_HEAD_DIM] = 1.0
    return pl.pallas_call(
        _ssd_kernel,
        grid=(bsz, nc),
        in_specs=[
            row(M2_WIDTH), row(M2_CONV_DIM),
            pl.BlockSpec((TAIL, M2_CONV_DIM), lambda b, c: (jnp.maximum((b * nc + c) * tail_blocks - 1, 0), 0)),
            row(LANES),
            _const_spec((CONV_K, M2_CONV_DIM)), _const_spec((1, M2_CONV_DIM)), _const_spec((1, LANES)), _const_spec((1, LANES)),
            _const_spec((1, M2_WIDTH)), _const_spec((1, M2_WIDTH)), _const_spec((LANES, M2_WIDTH)),
        ],
        out_specs=row(M2_WIDTH),
        out_shape=jax.ShapeDtypeStruct((t, M2_WIDTH), BF16),
        scratch_shapes=[
            pltpu.VMEM((TAIL + n, M2_CONV_DIM), F32),
            pltpu.VMEM((M2_GROUPS, M2_STATE, M2_GROUP_WIDTH), F32),
            pltpu.VMEM((n, M2_WIDTH), F32),
        ],
        compiler_params=_cparams("arbitrary", "arbitrary"),
        name="ssd",
    )(z, xbc, xbc, small, conv_w, conv_b.reshape(1, -1), gbias, alog_row, dskip_row, norm_w.reshape(1, -1), jnp.asarray(expand, BF16))


def _merge_ffn_kernel(x_ref, ys5_ref, yb_ref, yc_ref, g_ref, wglu_ref, wa_ref, wb_ref, wc_ref, wout_ref,
                      nmp_ref, nfp_ref, nfq_ref, wg_ref, wu_ref, wd_ref, out_ref):
    d = D_MODEL
    ys5 = ys5_ref[...]
    ya = ys5.astype(F32) * jax.nn.sigmoid(_dot(ys5, wglu_ref[...]))
    gate = lambda i: jax.nn.sigmoid(g_ref[:, i * d:(i + 1) * d].astype(F32))
    merged = (gate(0) * _dot(ya.astype(BF16), wa_ref[...]) + gate(1) * _dot(yb_ref[...], wb_ref[...])
              + gate(2) * _dot(yc_ref[...], wc_ref[...]))
    mix = _dot(merged.astype(BF16), wout_ref[...])
    x1 = x_ref[...] + _rms(mix, nmp_ref[...])
    hb = _rms(x1, nfp_ref[...]).astype(BF16)
    act = _silu(_dot(hb, wg_ref[...])) * _dot(hb, wu_ref[...])
    ffn = _dot(act.astype(BF16), wd_ref[...])
    out_ref[...] = x1 + _rms(ffn, nfq_ref[...])


def _merge_ffn(x2, ys5, yb, yc, gates, weights, norms):
    t = x2.shape[0]
    tm = min(TOKEN_BLOCK, t)
    row = lambda w: pl.BlockSpec((tm, w), lambda i: (i, 0))
    weights = [w.astype(BF16) for w in weights]
    norms = [v.reshape(1, D_MODEL) for v in norms]
    return pl.pallas_call(
        _merge_ffn_kernel,
        grid=(t // tm,),
        in_specs=[row(D_MODEL), row(S5_WIDTH), row(ML_WIDTH), row(M2_WIDTH), row(N_BRANCH * D_MODEL)]
        + [_const_spec(w.shape) for w in weights[:5]] + [_const_spec(v.shape) for v in norms]
        + [_const_spec(w.shape) for w in weights[5:]],
        out_specs=row(D_MODEL),
        out_shape=jax.ShapeDtypeStruct((t, D_MODEL), F32),
        compiler_params=_cparams("parallel"),
        name="merge_ffn",
    )(x2, ys5, yb, yc, gates, *weights[:5], *norms, *weights[5:])


def _gate_bias_row(bias_i, bias_f, dt_bias):
    pad = jnp.zeros((LANES - 2 * ML_HEADS - M2_HEADS,), F32)
    return jnp.concatenate([bias_i, bias_f, dt_bias, pad]).reshape(1, LANES)


def _head_lane_row(v):
    pad_l = jnp.zeros((SLOT_DT,), F32)
    pad_r = jnp.zeros((LANES - SLOT_DT - M2_HEADS,), F32)
    return jnp.concatenate([pad_l, v, pad_r]).reshape(1, LANES)


def _layer(x2, bsz, p):
    u, qk, v, o, z, xbc, gates, small = _in_proj(x2, p["norm_mix_pre"], _pack_w_in(p["w_in"]))
    mats = _s5_prep(p["s5_a_re"], p["s5_a_im"], p["s5_b_re"], p["s5_b_im"], p["s5_c_re"], p["s5_c_im"], p["s5_d"], p["s5_log_dt"])
    ys5 = _s5_mixer(u, bsz, mats)
    gbias = _gate_bias_row(p["ml_bias_i"], p["ml_bias_f"], p["m2_dt_bias"])
    yb = _mlstm(qk, v, o, small, p["ml_conv_w"], p["ml_conv_b"], gbias, bsz)
    yc = _ssd(z, xbc, small, p["m2_conv_w"], p["m2_conv_b"], gbias, _head_lane_row(p["m2_a_log"]),
              jnp.repeat(p["m2_d"], M2_HEAD_DIM).reshape(1, M2_WIDTH), p["m2_norm_w"], bsz)
    weights = [p["s5_w_glu"], p["w_br_a"], p["w_br_b"], p["w_br_c"], p["w_out"], p["w_ffn_gate"], p["w_ffn_up"], p["w_ffn_down"]]
    norms = [p["norm_mix_post"], p["norm_ffn_pre"], p["norm_ffn_post"]]
    return _merge_ffn(x2, ys5, yb, yc, gates, weights, norms)


_PARAM_NAMES = ("norm_mix_pre", "norm_mix_post", "w_in", "s5_a_re", "s5_a_im", "s5_b_re", "s5_b_im", "s5_c_re", "s5_c_im", "s5_d",
                "s5_log_dt", "s5_w_glu", "ml_conv_w", "ml_conv_b", "ml_bias_i", "ml_bias_f", "m2_conv_w", "m2_conv_b", "m2_dt_bias",
                "m2_a_log", "m2_d", "m2_norm_w", "w_br_a", "w_br_b", "w_br_c", "w_out", "norm_ffn_pre", "norm_ffn_post",
                "w_ffn_gate", "w_ffn_up", "w_ffn_down")


def kernel(x, norm_mix_pre, norm_mix_post, w_in, s5_a_re, s5_a_im, s5_b_re, s5_b_im, s5_c_re, s5_c_im, s5_d, s5_log_dt, s5_w_glu, ml_conv_w, ml_conv_b, ml_bias_i, ml_bias_f, m2_conv_w, m2_conv_b, m2_dt_bias, m2_a_log, m2_d, m2_norm_w, w_br_a, w_br_b, w_br_c, w_out, norm_ffn_pre, norm_ffn_post, w_ffn_gate, w_ffn_up, w_ffn_down):
    stacked = (norm_mix_pre, norm_mix_post, w_in, s5_a_re, s5_a_im, s5_b_re, s5_b_im, s5_c_re, s5_c_im, s5_d, s5_log_dt, s5_w_glu,
               ml_conv_w, ml_conv_b, ml_bias_i, ml_bias_f, m2_conv_w, m2_conv_b, m2_dt_bias, m2_a_log, m2_d, m2_norm_w,
               w_br_a, w_br_b, w_br_c, w_out, norm_ffn_pre, norm_ffn_post, w_ffn_gate, w_ffn_up, w_ffn_down)
    bsz, seq, d = x.shape
    x2 = x.reshape(bsz * seq, d)
    for layer in range(norm_mix_pre.shape[0]):
        x2 = _layer(x2, bsz, {name: arr[layer] for name, arr in zip(_PARAM_NAMES, stacked)})
    return x2.reshape(bsz, seq, d)
```

```python
import jax
import jax.numpy as jnp
import numpy as np
from jax import lax
from jax.experimental import pallas as pl
from jax.experimental.pallas import tpu as pltpu

F32 = jnp.float32
BF16 = jnp.bfloat16

D_MODEL = 1024
EPS = 1e-6
CONV_K = 4

S5_WIDTH = 512
S5_GROUP = 16
S5_GROUPS = S5_WIDTH // S5_GROUP
S5_STATE = 64
S5_CHUNK = 16
S5_K = S5_CHUNK * S5_GROUP
S5_CHUNK_BLOCK = 256

ML_WIDTH = 512
ML_HEADS = 4
ML_HEAD_DIM = ML_WIDTH // ML_HEADS

M2_WIDTH = 1024
M2_HEAD_DIM = 64
M2_HEADS = M2_WIDTH // M2_HEAD_DIM
M2_GROUPS = 2
M2_HPG = M2_HEADS // M2_GROUPS
M2_STATE = 128
M2_CONV_DIM = M2_WIDTH + 2 * M2_GROUPS * M2_STATE
M2_GROUP_WIDTH = M2_WIDTH // M2_GROUPS

N_BRANCH = 3
FFN_HIDDEN = -(-(8 * D_MODEL) // (3 * 256)) * 256

LANES = 128
SUBLANES = 8
TAIL = SUBLANES
GROUPS_PER_VREG = LANES // S5_GROUP

SLOT_I = 0
SLOT_F = ML_HEADS
SLOT_DT = 2 * ML_HEADS

SEQ_CHUNK = 128
TOKEN_BLOCK = 256

VMEM_LIMIT = 56 * 1024 * 1024


def _cparams(*sem):
    return pltpu.CompilerParams(dimension_semantics=sem, vmem_limit_bytes=VMEM_LIMIT)


def _const_spec(shape):
    nd = len(shape)
    return pl.BlockSpec(shape, lambda *_: (0,) * nd, pipeline_mode=pl.Buffered(1))


def _layer_spec(stacked, layer):
    shape = stacked.shape[1:]
    zeros = (0,) * len(shape)
    return pl.BlockSpec((None,) + shape, lambda *_: (layer,) + zeros, pipeline_mode=pl.Buffered(1))


def _rms(x, w):
    return x * lax.rsqrt(jnp.mean(x * x, axis=-1, keepdims=True) + EPS) * w


def _split3(x):
    hi = x.astype(BF16)
    r1 = x - hi.astype(F32)
    mid = r1.astype(BF16)
    lo = (r1 - mid.astype(F32)).astype(BF16)
    return hi, mid, lo


def _dot(a, b):
    return jnp.dot(a, b, preferred_element_type=F32)


def _dot_nt(a, b):
    return lax.dot_general(a, b, (((1,), (1,)), ((), ())), preferred_element_type=F32)


def _dot3_left(m_bf16, x):
    hi, mid, lo = _split3(x)
    return _dot(m_bf16, hi) + _dot(m_bf16, mid) + _dot(m_bf16, lo)


def _dot3_right(x, m_bf16):
    hi, mid, lo = _split3(x)
    return _dot(hi, m_bf16) + _dot(mid, m_bf16) + _dot(lo, m_bf16)


def _lower_tri(n):
    r = lax.broadcasted_iota(jnp.int32, (n, n), 0)
    c = lax.broadcasted_iota(jnp.int32, (n, n), 1)
    return r >= c


def _block_transpose(v):
    lane_blk = lax.broadcasted_iota(jnp.int32, v[0].shape, 1) // S5_GROUP
    b = GROUPS_PER_VREG // 2
    while b:
        bit_set = (lane_blk & b) != 0
        out = list(v)
        for r in range(GROUPS_PER_VREG):
            if r & b == 0:
                rp = r | b
                out[r] = jnp.where(bit_set, pltpu.roll(v[rp], b * S5_GROUP, 1), v[r])
                out[rp] = jnp.where(bit_set, v[rp], pltpu.roll(v[r], LANES - b * S5_GROUP, 1))
        v = out
        b //= 2
    return v


IN_SEGMENTS = (
    ("qk", 2 * ML_WIDTH, BF16),
    ("v", ML_WIDTH, BF16),
    ("o", ML_WIDTH, BF16),
    ("z", M2_WIDTH, BF16),
    ("xbc", M2_CONV_DIM, BF16),
    ("gates", N_BRANCH * D_MODEL, BF16),
    ("small", LANES, F32),
)
IN_PACKED = S5_WIDTH + sum(w for _, w, _ in IN_SEGMENTS)


def _pack_w_in(w_in):
    splits = (S5_WIDTH, ML_WIDTH, ML_WIDTH, ML_WIDTH, ML_WIDTH, ML_HEADS, ML_HEADS, M2_WIDTH, M2_CONV_DIM, M2_HEADS, N_BRANCH * D_MODEL)
    offs = np.cumsum((0,) + splits)
    u, q, k, v, o, i, f, z, xbc, dt, g = (w_in[..., offs[n]:offs[n + 1]].astype(BF16) for n in range(len(splits)))
    pad = jnp.zeros(w_in.shape[:-1] + (LANES - 2 * ML_HEADS - M2_HEADS,), BF16)
    return jnp.concatenate([u, q, k, v, o, z, xbc, g, i, f, dt, pad], axis=-1)


def _in_proj_kernel(x_ref, nw_ref, w_ref, ug_ref, *rest):
    out_refs, u_scr = rest[:-1], rest[-1]
    hb = _rms(x_ref[...], nw_ref[...]).astype(BF16)
    off = S5_WIDTH
    for ref, (_, width, _) in zip(out_refs, IN_SEGMENTS):
        ref[...] = _dot(hb, w_ref[:, off:off + width]).astype(ref.dtype)
        off += width

    nchunks = u_scr.shape[1] // S5_CHUNK
    for jb in range(S5_GROUPS // GROUPS_PER_VREG):
        u_scr[jb] = _dot(hb, w_ref[:, jb * LANES:(jb + 1) * LANES])
        for half in range(S5_CHUNK // GROUPS_PER_VREG):
            v = [u_scr[jb, pl.ds(half * GROUPS_PER_VREG + t, nchunks, stride=S5_CHUNK), :] for t in range(GROUPS_PER_VREG)]
            v = _block_transpose(v)
            for g in range(GROUPS_PER_VREG):
                ug_ref[jb * GROUPS_PER_VREG + g, :, half * LANES:(half + 1) * LANES] = v[g].astype(ug_ref.dtype)


def _in_proj(x2, norm_w, w_packed, layer):
    t = x2.shape[0]
    tm = min(TOKEN_BLOCK, t)
    nchunks = tm // S5_CHUNK
    return pl.pallas_call(
        _in_proj_kernel,
        grid=(t // tm,),
        in_specs=[
            pl.BlockSpec((tm, D_MODEL), lambda i: (i, 0)),
            _const_spec((1, D_MODEL)),
            _layer_spec(w_packed, layer),
        ],
        out_specs=[pl.BlockSpec((S5_GROUPS, nchunks, S5_K), lambda i: (0, i, 0))]
        + [pl.BlockSpec((tm, w), lambda i: (i, 0)) for _, w, _ in IN_SEGMENTS],
        out_shape=[jax.ShapeDtypeStruct((S5_GROUPS, t // S5_CHUNK, S5_K), BF16)]
        + [jax.ShapeDtypeStruct((t, w), dt) for _, w, dt in IN_SEGMENTS],
        scratch_shapes=[pltpu.VMEM((S5_WIDTH // LANES, tm, LANES), F32)],
        compiler_params=_cparams("parallel"),
        name="in_proj",
    )(x2, norm_w.reshape(1, D_MODEL), w_packed)


def _s5_prep_kernel(lr_ref, li_ref, logdt_ref, bt_ref, bti_ref, btr_t_ref, bti_t_ref, cr_ref, ci_ref, d_ref,
                    tt_ref, w_ref, mt_ref, pq_ref):
    p64 = S5_STATE
    lr, li = lr_ref[0], li_ref[0]
    dt = jnp.exp(logdt_ref[0])
    mag = jnp.exp(lr * dt)
    lb_re = mag * jnp.cos(li * dt)
    lb_im = mag * jnp.sin(li * dt)
    den = lr * lr + li * li
    f_re = ((lb_re - 1.0) * lr + lb_im * li) / den
    f_im = (lb_im * lr - (lb_re - 1.0) * li) / den

    pw = [(jnp.ones_like(lb_re), jnp.zeros_like(lb_im))]
    for _ in range(S5_CHUNK):
        pr, pi = pw[-1]
        pw.append((pr * lb_re - pi * lb_im, pr * lb_im + pi * lb_re))

    cr, ci = cr_ref[0], ci_ref[0]
    btr_t, bti_t = btr_t_ref[0], bti_t_ref[0]
    g_re, g_im, m_re, m_im, w_re, w_im = [], [], [], [], [], []
    for k in range(S5_CHUNK):
        pr, pi = pw[k]
        afr = pr * f_re - pi * f_im
        afi = pr * f_im + pi * f_re
        g_re.append(afr * cr - afi * ci)
        g_im.append(afr * ci + afi * cr)
        qr, qi = pw[k + 1]
        m_re.append(qr * cr - qi * ci)
        m_im.append(qr * ci + qi * cr)
        sr, si = pw[S5_CHUNK - 1 - k]
        wfr = sr * f_re - si * f_im
        wfi = sr * f_im + si * f_re
        w_re.append(wfr * btr_t - wfi * bti_t)
        w_im.append(wfr * bti_t + wfi * btr_t)
    cat = lambda blocks: jnp.concatenate(blocks, axis=0)
    w_ref[0, :, 0:p64] = cat(w_re).astype(w_ref.dtype)
    w_ref[0, :, p64:2 * p64] = cat(w_im).astype(w_ref.dtype)
    mt_ref[0, :, 0:p64] = cat(m_re).astype(mt_ref.dtype)
    mt_ref[0, :, p64:2 * p64] = (-cat(m_im)).astype(mt_ref.dtype)

    hi = lax.Precision.HIGHEST
    kk = (jnp.dot(cat(g_re), bt_ref[0], precision=hi, preferred_element_type=F32)
          - jnp.dot(cat(g_im), bti_ref[0], precision=hi, preferred_element_type=F32))
    rows = lax.broadcasted_iota(jnp.int32, (S5_K, S5_K), 0)
    cols = lax.broadcasted_iota(jnp.int32, (S5_K, S5_K), 1)
    col_blk = cols // S5_GROUP
    tt = jnp.where(rows == cols, d_ref[0], 0.0)
    for s in range(S5_CHUNK):
        if s == 0:
            shifted = kk
        else:
            shifted = jnp.concatenate([jnp.zeros((s * S5_GROUP, S5_K), F32), kk[:S5_K - s * S5_GROUP, :]], axis=0)
        tt = tt + jnp.where(col_blk == s, shifted, 0.0)
    tt_ref[0] = tt.astype(tt_ref.dtype)

    cr16, ci16 = pw[S5_CHUNK]
    pq_ref[0, 0:1, :] = jnp.concatenate([cr16, cr16], axis=1)
    pq_ref[0, 1:2, :] = jnp.concatenate([-ci16, ci16], axis=1)


def _s5_prep(a_re, a_im, b_re, b_im, c_re, c_im, d_skip, log_dt):
    g, p = S5_GROUPS, S5_STATE
    tile_lanes = lambda b: jnp.tile(b, (1, 1, S5_CHUNK))
    swap = lambda b: jnp.transpose(b, (0, 2, 1))
    args = (
        a_re.reshape(g, 1, p), a_im.reshape(g, 1, p), log_dt.reshape(g, 1, 1),
        tile_lanes(b_re), tile_lanes(b_im), swap(b_re), swap(b_im), c_re, c_im,
        jnp.tile(d_skip, (1, S5_CHUNK)).reshape(g, 1, S5_K),
    )
    spec = lambda a: pl.BlockSpec((1,) + a.shape[1:], lambda i: (i, 0, 0))
    out_shapes = (
        jax.ShapeDtypeStruct((g, S5_K, S5_K), BF16),
        jax.ShapeDtypeStruct((g, S5_K, 2 * p), BF16),
        jax.ShapeDtypeStruct((g, S5_K, 2 * p), BF16),
        jax.ShapeDtypeStruct((g, 2, 2 * p), F32),
    )
    return pl.pallas_call(
        _s5_prep_kernel,
        grid=(g,),
        in_specs=[spec(a) for a in args],
        out_specs=[spec(o) for o in out_shapes],
        out_shape=out_shapes,
        compiler_params=_cparams("parallel"),
        name="s5_prep",
    )(*args)


def _s5_kernel(ug_ref, tt_ref, w_ref, mt_ref, pq_ref, yg_ref, l_scr, s_scr, state_scr):
    ng = S5_GROUPS
    nchunks = ug_ref.shape[1]

    @pl.when(pl.program_id(1) == 0)
    def _():
        state_scr[...] = jnp.zeros_like(state_scr)

    def local(g, carry):
        l_scr[pl.ds(g, nchunks, stride=ng), :] = _dot(ug_ref[g], w_ref[g])
        return carry

    lax.fori_loop(0, ng, local, 0)

    p_mul = pq_ref[:, 0, :]
    q_mul = pq_ref[:, 1, :]

    def step(j, carry):
        x, xs = carry
        row = pl.multiple_of(j * ng, ng)
        s_scr[pl.ds(row, ng), :] = x
        l = l_scr[pl.ds(row, ng), :]
        ls = pltpu.roll(l, S5_STATE, 1)
        return p_mul * x + q_mul * xs + l, p_mul * xs - q_mul * x + ls

    x, xs = lax.fori_loop(0, nchunks, step, (state_scr[0], state_scr[1]), unroll=8)
    state_scr[0] = x
    state_scr[1] = xs

    def output(g, carry):
        u = ug_ref[g]
        s_in = s_scr[pl.ds(g, nchunks, stride=ng), :].astype(BF16)
        y = _dot_nt(u, tt_ref[g]) + _dot_nt(s_in, mt_ref[g])
        yg_ref[g] = jax.nn.gelu(y).astype(yg_ref.dtype)
        return carry

    lax.fori_loop(0, ng, output, 0)


def _s5_mixer(ug, bsz, mats):
    tt, w_mat, mt, pq = mats
    g, nch, _ = ug.shape
    nch_b = nch // bsz
    cb = min(S5_CHUNK_BLOCK, nch_b)
    nblk = nch_b // cb
    blk = pl.BlockSpec((g, cb, S5_K), lambda b, j: (0, b * nblk + j, 0))
    return pl.pallas_call(
        _s5_kernel,
        grid=(bsz, nblk),
        in_specs=[blk, _const_spec(tt.shape), _const_spec(w_mat.shape), _const_spec(mt.shape), _const_spec(pq.shape)],
        out_specs=blk,
        out_shape=jax.ShapeDtypeStruct(ug.shape, BF16),
        scratch_shapes=[
            pltpu.VMEM((cb * g, 2 * S5_STATE), F32),
            pltpu.VMEM((cb * g, 2 * S5_STATE), F32),
            pltpu.VMEM((2, g, 2 * S5_STATE), F32),
        ],
        compiler_params=_cparams("arbitrary", "arbitrary"),
        name="s5",
    )(ug, tt, w_mat, mt, pq)


def _causal_conv(x_ref, tail_ref, w_ref, b_ref, pad_scr, first):
    n = x_ref.shape[0]
    tail = tail_ref[...].astype(F32)
    pad_scr[0:TAIL, :] = jnp.where(first, 0.0, tail)
    pad_scr[TAIL:TAIL + n, :] = x_ref[...].astype(F32)
    acc = b_ref[...] + w_ref[CONV_K - 1:CONV_K, :] * pad_scr[TAIL:TAIL + n, :]
    for j in range(CONV_K - 1):
        acc = acc + w_ref[j:j + 1, :] * pad_scr[pl.ds(TAIL - (CONV_K - 1) + j, n), :]
    return acc


def _silu(x):
    return x * jax.nn.sigmoid(x)


def _mlstm_kernel(qk_ref, tail_ref, v_ref, o_ref, small_ref, cw_ref, cb_ref, gbias_ref, y_ref, pad_scr, ct_scr, m_scr):
    first = pl.program_id(1) == 0
    n = qk_ref.shape[0]
    dh = ML_HEAD_DIM

    @pl.when(first)
    def _():
        ct_scr[...] = jnp.zeros_like(ct_scr)
        m_scr[...] = jnp.zeros_like(m_scr)

    qk = _silu(_causal_conv(qk_ref, tail_ref, cw_ref, cb_ref, pad_scr, first))
    gate = small_ref[...] + gbias_ref[...]
    lf = jax.nn.log_sigmoid(gate)
    tri = _lower_tri(n)
    b_all = _dot3_left(tri.astype(BF16), lf)
    gate_t = gate.T
    b_t = b_all.T
    ones_col = (lax.broadcasted_iota(jnp.int32, (n, dh), 1) == 0).astype(BF16)

    for h in range(ML_HEADS):
        q = qk[:, h * dh:(h + 1) * dh].astype(BF16)
        k = qk[:, ML_WIDTH + h * dh:ML_WIDTH + (h + 1) * dh] * (dh ** -0.5)
        k_t = k.T.astype(BF16)
        v_ext = jnp.concatenate([v_ref[:, h * dh:(h + 1) * dh], ones_col], axis=1)
        b_col = b_all[:, SLOT_F + h:SLOT_F + h + 1]
        ig_col = gate[:, SLOT_I + h:SLOT_I + h + 1]
        b_row = b_t[SLOT_F + h:SLOT_F + h + 1, :]
        ig_row = gate_t[SLOT_I + h:SLOT_I + h + 1, :]
        m_prev = m_scr[h][:, 0:1]
        ct_prev = ct_scr[h]

        dmat = jnp.where(tri, b_col - b_row + ig_row, -jnp.inf)
        inter = b_col + m_prev
        m_t = jnp.maximum(jnp.max(dmat, axis=1, keepdims=True), inter)
        p = _dot(q, k_t) * jnp.exp(dmat - m_t)
        w_inter = jnp.exp(inter - m_t)
        num_den = _dot(p.astype(BF16), v_ext) + w_inter * _dot(q, ct_prev.astype(BF16))
        den = num_den[:, dh:dh + 1]
        denom = jnp.maximum(jnp.abs(den), jnp.exp(-m_t))
        out = num_den[:, :dh] / denom * jax.nn.sigmoid(o_ref[:, h * dh:(h + 1) * dh].astype(F32))
        y_ref[:, h * dh:(h + 1) * dh] = out.astype(y_ref.dtype)

        b_end = b_col[n - 1:n, :]
        a_end = b_end - b_col + ig_col
        m_loc = jnp.max(a_end, axis=0, keepdims=True)
        w_end = jnp.exp(a_end - m_loc)
        m_new = jnp.maximum(b_end + m_prev, m_loc)
        s_prev = jnp.exp(b_end + m_prev - m_new)
        s_loc = jnp.exp(m_loc - m_new)
        upd = _dot(k_t, (w_end * v_ext.astype(F32)).astype(BF16))
        ct_scr[h] = s_prev * ct_prev + s_loc * upd
        m_scr[h] = jnp.broadcast_to(m_new, m_scr.shape[1:])


def _mlstm(qk, v, o, small, conv_w, conv_b, gbias, bsz):
    t = qk.shape[0]
    seq = t // bsz
    n = min(SEQ_CHUNK, seq)
    nc = seq // n
    row = lambda w: pl.BlockSpec((n, w), lambda b, c: (b * nc + c, 0))
    tail_blocks = n // TAIL
    return pl.pallas_call(
        _mlstm_kernel,
        grid=(bsz, nc),
        in_specs=[
            row(2 * ML_WIDTH),
            pl.BlockSpec((TAIL, 2 * ML_WIDTH), lambda b, c: (jnp.maximum((b * nc + c) * tail_blocks - 1, 0), 0)),
            row(ML_WIDTH), row(ML_WIDTH), row(LANES),
            _const_spec((CONV_K, 2 * ML_WIDTH)), _const_spec((1, 2 * ML_WIDTH)), _const_spec((1, LANES)),
        ],
        out_specs=row(ML_WIDTH),
        out_shape=jax.ShapeDtypeStruct((t, ML_WIDTH), BF16),
        scratch_shapes=[
            pltpu.VMEM((TAIL + n, 2 * ML_WIDTH), F32),
            pltpu.VMEM((ML_HEADS, ML_HEAD_DIM, 2 * ML_HEAD_DIM), F32),
            pltpu.VMEM((ML_HEADS, 1, LANES), F32),
        ],
        compiler_params=_cparams("arbitrary", "arbitrary"),
        name="mlstm",
    )(qk, qk, v, o, small, conv_w, conv_b.reshape(1, -1), gbias)


def _ssd_kernel(z_ref, xbc_ref, tail_ref, small_ref, cw_ref, cb_ref, gbias_ref, alog_ref, dskip_ref, normw_ref, expand_ref,
                y_ref, pad_scr, state_scr, ydiag_scr):
    first = pl.program_id(1) == 0
    n = z_ref.shape[0]
    gw = M2_GROUP_WIDTH
    hd = M2_HEAD_DIM

    @pl.when(first)
    def _():
        state_scr[...] = jnp.zeros_like(state_scr)

    xbc = _silu(_causal_conv(xbc_ref, tail_ref, cw_ref, cb_ref, pad_scr, first))
    xs = xbc[:, :M2_WIDTH]

    lane = lax.broadcasted_iota(jnp.int32, (1, LANES), 1)
    dt_lanes = (lane >= SLOT_DT) & (lane < SLOT_DT + M2_HEADS)
    dt = jnp.where(dt_lanes, jax.nn.softplus(small_ref[...] + gbias_ref[...]), 0.0)
    a_row = jnp.where(dt_lanes, -jnp.exp(alog_ref[...]), 0.0)
    tri = _lower_tri(n)
    cum = _dot3_left(tri.astype(BF16), dt * a_row)
    cum_t = cum.T
    expand = expand_ref[...]
    dt_e = _dot3_right(dt, expand)
    cum_e = _dot3_right(cum, expand)
    cum_end_e = cum_e[n - 1:n, :]
    xdt = xs * dt_e
    x_in = (xdt * jnp.exp(cum_end_e - cum_e)).astype(BF16)
    decay_in = jnp.exp(cum_e)
    decay_chunk = jnp.exp(cum_end_e)
    xdt_b = xdt.astype(BF16)

    for g in range(M2_GROUPS):
        b_g = xbc[:, M2_WIDTH + g * M2_STATE:M2_WIDTH + (g + 1) * M2_STATE]
        c_g = xbc[:, M2_WIDTH + (M2_GROUPS + g) * M2_STATE:M2_WIDTH + (M2_GROUPS + g + 1) * M2_STATE].astype(BF16)
        b_t = b_g.T.astype(BF16)
        cb = _dot(c_g, b_t)
        for r in range(M2_HPG):
            head = g * M2_HPG + r
            cum_col = cum[:, SLOT_DT + head:SLOT_DT + head + 1]
            cum_row = cum_t[SLOT_DT + head:SLOT_DT + head + 1, :]
            w = cb * jnp.exp(jnp.where(tri, cum_col - cum_row, -jnp.inf))
            ydiag_scr[:, head * hd:(head + 1) * hd] = _dot(w.astype(BF16), xdt_b[:, head * hd:(head + 1) * hd])
        cols = slice(g * gw, (g + 1) * gw)
        st = state_scr[g]
        y_off = _dot(c_g, st.astype(BF16)) * decay_in[:, cols]
        y = ydiag_scr[:, cols] + y_off + dskip_ref[:, cols] * xs[:, cols]
        y = y * _silu(z_ref[:, cols].astype(F32))
        y = y * lax.rsqrt(jnp.mean(y * y, axis=-1, keepdims=True) + EPS)
        y_ref[:, cols] = (y * normw_ref[:, cols]).astype(y_ref.dtype)
        state_scr[g] = st * decay_chunk[:, cols] + _dot(b_t, x_in[:, cols])


def _ssd(z, xbc, small, conv_w, conv_b, gbias, alog_row, dskip_row, norm_w, bsz):
    t = z.shape[0]
    seq = t // bsz
    n = min(SEQ_CHUNK, seq)
    nc = seq // n
    row = lambda w: pl.BlockSpec((n, w), lambda b, c: (b * nc + c, 0))
    tail_blocks = n // TAIL
    expand = np.zeros((LANES, M2_WIDTH), np.float32)
    for h in range(M2_HEADS):
        expand[SLOT_DT + h, h * M2_HEAD_DIM:(h + 1) * M2_HEAD_DIM] = 1.0
    return pl.pallas_call(
        _ssd_kernel,
        grid=(bsz, nc),
        in_specs=[
            row(M2_WIDTH), row(M2_CONV_DIM),
            pl.BlockSpec((TAIL, M2_CONV_DIM), lambda b, c: (jnp.maximum((b * nc + c) * tail_blocks - 1, 0), 0)),
            row(LANES),
            _const_spec((CONV_K, M2_CONV_DIM)), _const_spec((1, M2_CONV_DIM)), _const_spec((1, LANES)), _const_spec((1, LANES)),
            _const_spec((1, M2_WIDTH)), _const_spec((1, M2_WIDTH)), _const_spec((LANES, M2_WIDTH)),
        ],
        out_specs=row(M2_WIDTH),
        out_shape=jax.ShapeDtypeStruct((t, M2_WIDTH), BF16),
        scratch_shapes=[
            pltpu.VMEM((TAIL + n, M2_CONV_DIM), F32),
            pltpu.VMEM((M2_GROUPS, M2_STATE, M2_GROUP_WIDTH), F32),
            pltpu.VMEM((n, M2_WIDTH), F32),
        ],
        compiler_params=_cparams("arbitrary", "arbitrary"),
        name="ssd",
    )(z, xbc, xbc, small, conv_w, conv_b.reshape(1, -1), gbias, alog_row, dskip_row, norm_w.reshape(1, -1), jnp.asarray(expand, BF16))


def _merge_ffn_kernel(x_ref, yg_ref, yb_ref, yc_ref, g_ref, wglu_ref, wa_ref, wb_ref, wc_ref, wout_ref,
                      nmp_ref, nfp_ref, nfq_ref, wg_ref, wu_ref, wd_ref, out_ref, ys_scr):
    d = D_MODEL
    nchunks = yg_ref.shape[1]
    for jb in range(S5_GROUPS // GROUPS_PER_VREG):
        for half in range(S5_CHUNK // GROUPS_PER_VREG):
            v = [yg_ref[jb * GROUPS_PER_VREG + g, :, half * LANES:(half + 1) * LANES].astype(F32) for g in range(GROUPS_PER_VREG)]
            v = _block_transpose(v)
            for t in range(GROUPS_PER_VREG):
                ys_scr[jb, pl.ds(half * GROUPS_PER_VREG + t, nchunks, stride=S5_CHUNK), :] = v[t]
    ys5 = jnp.concatenate([ys_scr[jb] for jb in range(S5_WIDTH // LANES)], axis=1)
    ya = ys5 * jax.nn.sigmoid(_dot(ys5.astype(BF16), wglu_ref[...]))
    gate = lambda i: jax.nn.sigmoid(g_ref[:, i * d:(i + 1) * d].astype(F32))
    merged = (gate(0) * _dot(ya.astype(BF16), wa_ref[...]) + gate(1) * _dot(yb_ref[...], wb_ref[...])
              + gate(2) * _dot(yc_ref[...], wc_ref[...]))
    mix = _dot(merged.astype(BF16), wout_ref[...])
    x1 = x_ref[...] + _rms(mix, nmp_ref[...])
    hb = _rms(x1, nfp_ref[...]).astype(BF16)
    act = _silu(_dot(hb, wg_ref[...])) * _dot(hb, wu_ref[...])
    ffn = _dot(act.astype(BF16), wd_ref[...])
    out_ref[...] = x1 + _rms(ffn, nfq_ref[...])


def _merge_ffn(x2, yg, yb, yc, gates, weights, norms, layer):
    t = x2.shape[0]
    tm = min(TOKEN_BLOCK, t)
    row = lambda w: pl.BlockSpec((tm, w), lambda i: (i, 0))
    norms = [v.reshape(1, D_MODEL) for v in norms]
    return pl.pallas_call(
        _merge_ffn_kernel,
        grid=(t // tm,),
        in_specs=[row(D_MODEL), pl.BlockSpec((S5_GROUPS, tm // S5_CHUNK, S5_K), lambda i: (0, i, 0)),
                  row(ML_WIDTH), row(M2_WIDTH), row(N_BRANCH * D_MODEL)]
        + [_layer_spec(w, layer) for w in weights[:5]] + [_const_spec(v.shape) for v in norms]
        + [_layer_spec(w, layer) for w in weights[5:]],
        out_specs=row(D_MODEL),
        out_shape=jax.ShapeDtypeStruct((t, D_MODEL), F32),
        scratch_shapes=[pltpu.VMEM((S5_WIDTH // LANES, tm, LANES), F32)],
        compiler_params=_cparams("parallel"),
        name="merge_ffn",
    )(x2, yg, yb, yc, gates, *weights[:5], *norms, *weights[5:])


def _gate_bias_row(bias_i, bias_f, dt_bias):
    pad = jnp.zeros((LANES - 2 * ML_HEADS - M2_HEADS,), F32)
    return jnp.concatenate([bias_i, bias_f, dt_bias, pad]).reshape(1, LANES)


def _head_lane_row(v):
    pad_l = jnp.zeros((SLOT_DT,), F32)
    pad_r = jnp.zeros((LANES - SLOT_DT - M2_HEADS,), F32)
    return jnp.concatenate([pad_l, v, pad_r]).reshape(1, LANES)


def _layer(x2, bsz, p, big, layer):
    ug, qk, v, o, z, xbc, gates, small = _in_proj(x2, p["norm_mix_pre"], big["w_in"], layer)
    mats = _s5_prep(p["s5_a_re"], p["s5_a_im"], p["s5_b_re"], p["s5_b_im"], p["s5_c_re"], p["s5_c_im"], p["s5_d"], p["s5_log_dt"])
    yg = _s5_mixer(ug, bsz, mats)
    gbias = _gate_bias_row(p["ml_bias_i"], p["ml_bias_f"], p["m2_dt_bias"])
    yb = _mlstm(qk, v, o, small, p["ml_conv_w"], p["ml_conv_b"], gbias, bsz)
    yc = _ssd(z, xbc, small, p["m2_conv_w"], p["m2_conv_b"], gbias, _head_lane_row(p["m2_a_log"]),
              jnp.repeat(p["m2_d"], M2_HEAD_DIM).reshape(1, M2_WIDTH), p["m2_norm_w"], bsz)
    norms = [p["norm_mix_post"], p["norm_ffn_pre"], p["norm_ffn_post"]]
    return _merge_ffn(x2, yg, yb, yc, gates, big["merge_ffn"], norms, layer)


_PARAM_NAMES = ("norm_mix_pre", "norm_mix_post", "w_in", "s5_a_re", "s5_a_im", "s5_b_re", "s5_b_im", "s5_c_re", "s5_c_im", "s5_d",
                "s5_log_dt", "s5_w_glu", "ml_conv_w", "ml_conv_b", "ml_bias_i", "ml_bias_f", "m2_conv_w", "m2_conv_b", "m2_dt_bias",
                "m2_a_log", "m2_d", "m2_norm_w", "w_br_a", "w_br_b", "w_br_c", "w_out", "norm_ffn_pre", "norm_ffn_post",
                "w_ffn_gate", "w_ffn_up", "w_ffn_down")


def kernel(x, norm_mix_pre, norm_mix_post, w_in, s5_a_re, s5_a_im, s5_b_re, s5_b_im, s5_c_re, s5_c_im, s5_d, s5_log_dt, s5_w_glu, ml_conv_w, ml_conv_b, ml_bias_i, ml_bias_f, m2_conv_w, m2_conv_b, m2_dt_bias, m2_a_log, m2_d, m2_norm_w, w_br_a, w_br_b, w_br_c, w_out, norm_ffn_pre, norm_ffn_post, w_ffn_gate, w_ffn_up, w_ffn_down):
    stacked = (norm_mix_pre, norm_mix_post, w_in, s5_a_re, s5_a_im, s5_b_re, s5_b_im, s5_c_re, s5_c_im, s5_d, s5_log_dt, s5_w_glu,
               ml_conv_w, ml_conv_b, ml_bias_i, ml_bias_f, m2_conv_w, m2_conv_b, m2_dt_bias, m2_a_log, m2_d, m2_norm_w,
               w_br_a, w_br_b, w_br_c, w_out, norm_ffn_pre, norm_ffn_post, w_ffn_gate, w_ffn_up, w_ffn_down)
    bsz, seq, d = x.shape
    x2 = x.reshape(bsz * seq, d)
    big = {
        "w_in": _pack_w_in(w_in),
        "merge_ffn": [w.astype(BF16) for w in (s5_w_glu, w_br_a, w_br_b, w_br_c, w_out, w_ffn_gate, w_ffn_up, w_ffn_down)],
    }
    for layer in range(norm_mix_pre.shape[0]):
        x2 = _layer(x2, bsz, {name: arr[layer] for name, arr in zip(_PARAM_NAMES, stacked)}, big, layer)
    return x2.reshape(bsz, seq, d)
```

```python
import jax
import jax.numpy as jnp
import numpy as np
from jax import lax
from jax.experimental import pallas as pl
from jax.experimental.pallas import tpu as pltpu

F32 = jnp.float32
BF16 = jnp.bfloat16

D_MODEL = 1024
EPS = 1e-6
CONV_K = 4

S5_WIDTH = 512
S5_GROUP = 16
S5_GROUPS = S5_WIDTH // S5_GROUP
S5_STATE = 64
S5_CHUNK = 16
S5_K = S5_CHUNK * S5_GROUP
S5_CHUNK_BLOCK = 256

ML_WIDTH = 512
ML_HEADS = 4
ML_HEAD_DIM = ML_WIDTH // ML_HEADS

M2_WIDTH = 1024
M2_HEAD_DIM = 64
M2_HEADS = M2_WIDTH // M2_HEAD_DIM
M2_GROUPS = 2
M2_HPG = M2_HEADS // M2_GROUPS
M2_STATE = 128
M2_CONV_DIM = M2_WIDTH + 2 * M2_GROUPS * M2_STATE
M2_GROUP_WIDTH = M2_WIDTH // M2_GROUPS

N_BRANCH = 3
FFN_HIDDEN = -(-(8 * D_MODEL) // (3 * 256)) * 256

LANES = 128
SUBLANES = 8
TAIL = SUBLANES
GROUPS_PER_VREG = LANES // S5_GROUP

SLOT_I = 0
SLOT_F = ML_HEADS
SLOT_DT = 2 * ML_HEADS

SEQ_CHUNK = 128
TOKEN_BLOCK = 512

VMEM_LIMIT = 56 * 1024 * 1024


def _cparams(*sem):
    return pltpu.CompilerParams(dimension_semantics=sem, vmem_limit_bytes=VMEM_LIMIT)


def _const_spec(shape):
    nd = len(shape)
    return pl.BlockSpec(shape, lambda *_: (0,) * nd, pipeline_mode=pl.Buffered(1))


def _layer_spec(stacked, layer):
    shape = stacked.shape[1:]
    zeros = (0,) * len(shape)
    return pl.BlockSpec((None,) + shape, lambda *_: (layer,) + zeros, pipeline_mode=pl.Buffered(1))


def _rms(x, w):
    return x * lax.rsqrt(jnp.mean(x * x, axis=-1, keepdims=True) + EPS) * w


def _split3(x):
    hi = x.astype(BF16)
    r1 = x - hi.astype(F32)
    mid = r1.astype(BF16)
    lo = (r1 - mid.astype(F32)).astype(BF16)
    return hi, mid, lo


def _dot(a, b):
    return jnp.dot(a, b, preferred_element_type=F32)


def _dot_nt(a, b):
    return lax.dot_general(a, b, (((1,), (1,)), ((), ())), preferred_element_type=F32)


def _dot3_left(m_bf16, x):
    hi, mid, lo = _split3(x)
    return _dot(m_bf16, hi) + _dot(m_bf16, mid) + _dot(m_bf16, lo)


def _dot3_right(x, m_bf16):
    hi, mid, lo = _split3(x)
    return _dot(hi, m_bf16) + _dot(mid, m_bf16) + _dot(lo, m_bf16)


def _lower_tri(n):
    r = lax.broadcasted_iota(jnp.int32, (n, n), 0)
    c = lax.broadcasted_iota(jnp.int32, (n, n), 1)
    return r >= c


def _block_transpose(v):
    lane_blk = lax.broadcasted_iota(jnp.int32, v[0].shape, 1) // S5_GROUP
    b = GROUPS_PER_VREG // 2
    while b:
        bit_set = (lane_blk & b) != 0
        out = list(v)
        for r in range(GROUPS_PER_VREG):
            if r & b == 0:
                rp = r | b
                out[r] = jnp.where(bit_set, pltpu.roll(v[rp], b * S5_GROUP, 1), v[r])
                out[rp] = jnp.where(bit_set, v[rp], pltpu.roll(v[r], LANES - b * S5_GROUP, 1))
        v = out
        b //= 2
    return v


IN_SEGMENTS = (
    ("qk", 2 * ML_WIDTH, BF16),
    ("v", ML_WIDTH, BF16),
    ("o", ML_WIDTH, BF16),
    ("z", M2_WIDTH, BF16),
    ("xbc", M2_CONV_DIM, BF16),
    ("gates", N_BRANCH * D_MODEL, BF16),
    ("small", LANES, F32),
)
W_A_WIDTH = S5_WIDTH + 4 * ML_WIDTH


def _split_w_in(w_in):
    splits = (S5_WIDTH, ML_WIDTH, ML_WIDTH, ML_WIDTH, ML_WIDTH, ML_HEADS, ML_HEADS, M2_WIDTH, M2_CONV_DIM, M2_HEADS, N_BRANCH * D_MODEL)
    offs = np.cumsum((0,) + splits)
    col = lambda n0, n1: w_in[..., offs[n0]:offs[n1]].astype(BF16)
    pad = jnp.zeros(w_in.shape[:-1] + (LANES - 2 * ML_HEADS - M2_HEADS,), BF16)
    small = jnp.concatenate([col(5, 7), col(9, 10), pad], axis=-1)
    return [col(0, 5), col(7, 8), col(8, 9), col(10, 11), small]


def _in_proj_kernel(x_ref, nw_ref, wa_ref, wz_ref, wxbc_ref, wg_ref, wsmall_ref,
                    ug_ref, qk_ref, v_ref, o_ref, z_ref, xbc_ref, g_ref, small_ref, u_scr):
    hb = _rms(x_ref[...], nw_ref[...]).astype(BF16)
    off = S5_WIDTH
    for ref in (qk_ref, v_ref, o_ref):
        width = ref.shape[1]
        ref[...] = _dot(hb, wa_ref[:, off:off + width]).astype(ref.dtype)
        off += width
    for ref, w_ref in ((z_ref, wz_ref), (xbc_ref, wxbc_ref), (g_ref, wg_ref), (small_ref, wsmall_ref)):
        ref[...] = _dot(hb, w_ref[...]).astype(ref.dtype)

    u = _dot(hb, wa_ref[:, 0:S5_WIDTH])
    nchunks = u_scr.shape[1] // S5_CHUNK
    for jb in range(S5_GROUPS // GROUPS_PER_VREG):
        u_scr[jb] = u[:, jb * LANES:(jb + 1) * LANES]
        for half in range(S5_CHUNK // GROUPS_PER_VREG):
            v = [u_scr[jb, pl.ds(half * GROUPS_PER_VREG + t, nchunks, stride=S5_CHUNK), :] for t in range(GROUPS_PER_VREG)]
            v = _block_transpose(v)
            for g in range(GROUPS_PER_VREG):
                ug_ref[jb * GROUPS_PER_VREG + g, :, half * LANES:(half + 1) * LANES] = v[g].astype(ug_ref.dtype)


def _in_proj(x2, norm_w, w_parts, layer):
    t = x2.shape[0]
    tm = min(TOKEN_BLOCK, t)
    nchunks = tm // S5_CHUNK
    return pl.pallas_call(
        _in_proj_kernel,
        grid=(t // tm,),
        in_specs=[pl.BlockSpec((tm, D_MODEL), lambda i: (i, 0)), _const_spec((1, D_MODEL))]
        + [_layer_spec(w, layer) for w in w_parts],
        out_specs=[pl.BlockSpec((S5_GROUPS, nchunks, S5_K), lambda i: (0, i, 0))]
        + [pl.BlockSpec((tm, w), lambda i: (i, 0)) for _, w, _ in IN_SEGMENTS],
        out_shape=[jax.ShapeDtypeStruct((S5_GROUPS, t // S5_CHUNK, S5_K), BF16)]
        + [jax.ShapeDtypeStruct((t, w), dt) for _, w, dt in IN_SEGMENTS],
        scratch_shapes=[pltpu.VMEM((S5_WIDTH // LANES, tm, LANES), F32)],
        compiler_params=_cparams("parallel"),
        name="in_proj",
    )(x2, norm_w.reshape(1, D_MODEL), *w_parts)


def _s5_prep_kernel(lr_ref, li_ref, logdt_ref, bt_ref, bti_ref, btr_t_ref, bti_t_ref, cr_ref, ci_ref, d_ref,
                    tt_ref, w_ref, mt_ref, pq_ref):
    p64 = S5_STATE
    lr, li = lr_ref[0], li_ref[0]
    dt = jnp.exp(logdt_ref[0])
    mag = jnp.exp(lr * dt)
    lb_re = mag * jnp.cos(li * dt)
    lb_im = mag * jnp.sin(li * dt)
    den = lr * lr + li * li
    f_re = ((lb_re - 1.0) * lr + lb_im * li) / den
    f_im = (lb_im * lr - (lb_re - 1.0) * li) / den

    pw = [(jnp.ones_like(lb_re), jnp.zeros_like(lb_im))]
    for _ in range(S5_CHUNK):
        pr, pi = pw[-1]
        pw.append((pr * lb_re - pi * lb_im, pr * lb_im + pi * lb_re))

    cr, ci = cr_ref[0], ci_ref[0]
    btr_t, bti_t = btr_t_ref[0], bti_t_ref[0]
    g_re, g_im, m_re, m_im, w_re, w_im = [], [], [], [], [], []
    for k in range(S5_CHUNK):
        pr, pi = pw[k]
        afr = pr * f_re - pi * f_im
        afi = pr * f_im + pi * f_re
        g_re.append(afr * cr - afi * ci)
        g_im.append(afr * ci + afi * cr)
        qr, qi = pw[k + 1]
        m_re.append(qr * cr - qi * ci)
        m_im.append(qr * ci + qi * cr)
        sr, si = pw[S5_CHUNK - 1 - k]
        wfr = sr * f_re - si * f_im
        wfi = sr * f_im + si * f_re
        w_re.append(wfr * btr_t - wfi * bti_t)
        w_im.append(wfr * bti_t + wfi * btr_t)
    cat = lambda blocks: jnp.concatenate(blocks, axis=0)
    w_ref[0, :, 0:p64] = cat(w_re).astype(w_ref.dtype)
    w_ref[0, :, p64:2 * p64] = cat(w_im).astype(w_ref.dtype)
    mt_ref[0, :, 0:p64] = cat(m_re).astype(mt_ref.dtype)
    mt_ref[0, :, p64:2 * p64] = (-cat(m_im)).astype(mt_ref.dtype)

    hi = lax.Precision.HIGHEST
    kk = (jnp.dot(cat(g_re), bt_ref[0], precision=hi, preferred_element_type=F32)
          - jnp.dot(cat(g_im), bti_ref[0], precision=hi, preferred_element_type=F32))
    rows = lax.broadcasted_iota(jnp.int32, (S5_K, S5_K), 0)
    cols = lax.broadcasted_iota(jnp.int32, (S5_K, S5_K), 1)
    col_blk = cols // S5_GROUP
    tt = jnp.where(rows == cols, d_ref[0], 0.0)
    for s in range(S5_CHUNK):
        if s == 0:
            shifted = kk
        else:
            shifted = jnp.concatenate([jnp.zeros((s * S5_GROUP, S5_K), F32), kk[:S5_K - s * S5_GROUP, :]], axis=0)
        tt = tt + jnp.where(col_blk == s, shifted, 0.0)
    tt_ref[0] = tt.astype(tt_ref.dtype)

    cr16, ci16 = pw[S5_CHUNK]
    pq_ref[0, 0:1, :] = jnp.concatenate([cr16, cr16], axis=1)
    pq_ref[0, 1:2, :] = jnp.concatenate([-ci16, ci16], axis=1)


def _s5_prep(a_re, a_im, b_re, b_im, c_re, c_im, d_skip, log_dt):
    g, p = S5_GROUPS, S5_STATE
    tile_lanes = lambda b: jnp.tile(b, (1, 1, S5_CHUNK))
    swap = lambda b: jnp.transpose(b, (0, 2, 1))
    args = (
        a_re.reshape(g, 1, p), a_im.reshape(g, 1, p), log_dt.reshape(g, 1, 1),
        tile_lanes(b_re), tile_lanes(b_im), swap(b_re), swap(b_im), c_re, c_im,
        jnp.tile(d_skip, (1, S5_CHUNK)).reshape(g, 1, S5_K),
    )
    spec = lambda a: pl.BlockSpec((1,) + a.shape[1:], lambda i: (i, 0, 0))
    out_shapes = (
        jax.ShapeDtypeStruct((g, S5_K, S5_K), BF16),
        jax.ShapeDtypeStruct((g, S5_K, 2 * p), BF16),
        jax.ShapeDtypeStruct((g, S5_K, 2 * p), BF16),
        jax.ShapeDtypeStruct((g, 2, 2 * p), F32),
    )
    return pl.pallas_call(
        _s5_prep_kernel,
        grid=(g,),
        in_specs=[spec(a) for a in args],
        out_specs=[spec(o) for o in out_shapes],
        out_shape=out_shapes,
        compiler_params=_cparams("parallel"),
        name="s5_prep",
    )(*args)


def _s5_kernel(ug_ref, tt_ref, w_ref, mt_ref, pq_ref, yg_ref, l_scr, s_scr, state_scr):
    ng = S5_GROUPS
    nchunks = ug_ref.shape[1]

    @pl.when(pl.program_id(1) == 0)
    def _():
        state_scr[...] = jnp.zeros_like(state_scr)

    def local(g, carry):
        l_scr[pl.ds(g, nchunks, stride=ng), :] = _dot(ug_ref[g], w_ref[g])
        return carry

    lax.fori_loop(0, ng, local, 0, unroll=2)

    p_mul = pq_ref[0]
    q_mul = pq_ref[1]

    def step(j, carry):
        x, xs = carry
        row = pl.multiple_of(j * ng, ng)
        s_scr[pl.ds(row, ng), :] = x
        l = l_scr[pl.ds(row, ng), :]
        ls = pltpu.roll(l, S5_STATE, 1)
        return p_mul * x + q_mul * xs + l, p_mul * xs - q_mul * x + ls

    x, xs = lax.fori_loop(0, nchunks, step, (state_scr[0], state_scr[1]), unroll=8)
    state_scr[0] = x
    state_scr[1] = xs

    def output(g, carry):
        u = ug_ref[g]
        s_in = s_scr[pl.ds(g, nchunks, stride=ng), :].astype(BF16)
        y = _dot_nt(u, tt_ref[g]) + _dot_nt(s_in, mt_ref[g])
        yg_ref[g] = jax.nn.gelu(y).astype(yg_ref.dtype)
        return carry

    lax.fori_loop(0, ng, output, 0, unroll=2)


def _s5_mixer(ug, bsz, mats):
    tt, w_mat, mt, pq = mats
    pq = jnp.transpose(pq, (1, 0, 2))
    g, nch, _ = ug.shape
    nch_b = nch // bsz
    cb = min(S5_CHUNK_BLOCK, nch_b)
    nblk = nch_b // cb
    blk = pl.BlockSpec((g, cb, S5_K), lambda b, j: (0, b * nblk + j, 0))
    return pl.pallas_call(
        _s5_kernel,
        grid=(bsz, nblk),
        in_specs=[blk, _const_spec(tt.shape), _const_spec(w_mat.shape), _const_spec(mt.shape), _const_spec(pq.shape)],
        out_specs=blk,
        out_shape=jax.ShapeDtypeStruct(ug.shape, BF16),
        scratch_shapes=[
            pltpu.VMEM((cb * g, 2 * S5_STATE), F32),
            pltpu.VMEM((cb * g, 2 * S5_STATE), F32),
            pltpu.VMEM((2, g, 2 * S5_STATE), F32),
        ],
        compiler_params=_cparams("arbitrary", "arbitrary"),
        name="s5",
    )(ug, tt, w_mat, mt, pq)


def _causal_conv(x_ref, tail_ref, w_ref, b_ref, pad_scr, first):
    n = x_ref.shape[0]
    tail = tail_ref[...].astype(F32)
    pad_scr[0:TAIL, :] = jnp.where(first, 0.0, tail)
    pad_scr[TAIL:TAIL + n, :] = x_ref[...].astype(F32)
    acc = b_ref[...] + w_ref[CONV_K - 1:CONV_K, :] * pad_scr[TAIL:TAIL + n, :]
    for j in range(CONV_K - 1):
        acc = acc + w_ref[j:j + 1, :] * pad_scr[pl.ds(TAIL - (CONV_K - 1) + j, n), :]
    return acc


def _silu(x):
    return x * jax.nn.sigmoid(x)


def _mlstm_kernel(qk_ref, tail_ref, v_ref, o_ref, small_ref, cw_ref, cb_ref, gbias_ref, y_ref, pad_scr, ct_scr, m_scr):
    first = pl.program_id(1) == 0
    n = qk_ref.shape[0]
    dh = ML_HEAD_DIM

    @pl.when(first)
    def _():
        ct_scr[...] = jnp.zeros_like(ct_scr)
        m_scr[...] = jnp.zeros_like(m_scr)

    qk = _silu(_causal_conv(qk_ref, tail_ref, cw_ref, cb_ref, pad_scr, first))
    gate = small_ref[...] + gbias_ref[...]
    lf = jax.nn.log_sigmoid(gate)
    tri = _lower_tri(n)
    b_all = _dot3_left(tri.astype(BF16), lf)
    gate_t = gate.T
    b_t = b_all.T
    ones_col = (lax.broadcasted_iota(jnp.int32, (n, dh), 1) == 0).astype(BF16)

    for h in range(ML_HEADS):
        q = qk[:, h * dh:(h + 1) * dh].astype(BF16)
        k = qk[:, ML_WIDTH + h * dh:ML_WIDTH + (h + 1) * dh] * (dh ** -0.5)
        k_t = k.T.astype(BF16)
        v_ext = jnp.concatenate([v_ref[:, h * dh:(h + 1) * dh], ones_col], axis=1)
        b_col = b_all[:, SLOT_F + h:SLOT_F + h + 1]
        ig_col = gate[:, SLOT_I + h:SLOT_I + h + 1]
        b_row = b_t[SLOT_F + h:SLOT_F + h + 1, :]
        ig_row = gate_t[SLOT_I + h:SLOT_I + h + 1, :]
        m_prev = m_scr[h][:, 0:1]
        ct_prev = ct_scr[h]

        dmat = jnp.where(tri, b_col - b_row + ig_row, -jnp.inf)
        inter = b_col + m_prev
        m_t = jnp.maximum(jnp.max(dmat, axis=1, keepdims=True), inter)
        p = _dot(q, k_t) * jnp.exp(dmat - m_t)
        w_inter = jnp.exp(inter - m_t)
        num_den = _dot(p.astype(BF16), v_ext) + w_inter * _dot(q, ct_prev.astype(BF16))
        den = num_den[:, dh:dh + 1]
        denom = jnp.maximum(jnp.abs(den), jnp.exp(-m_t))
        out = num_den[:, :dh] / denom * jax.nn.sigmoid(o_ref[:, h * dh:(h + 1) * dh].astype(F32))
        y_ref[:, h * dh:(h + 1) * dh] = out.astype(y_ref.dtype)

        b_end = b_col[n - 1:n, :]
        a_end = b_end - b_col + ig_col
        m_loc = jnp.max(a_end, axis=0, keepdims=True)
        w_end = jnp.exp(a_end - m_loc)
        m_new = jnp.maximum(b_end + m_prev, m_loc)
        s_prev = jnp.exp(b_end + m_prev - m_new)
        s_loc = jnp.exp(m_loc - m_new)
        upd = _dot(k_t, (w_end * v_ext.astype(F32)).astype(BF16))
        ct_scr[h] = s_prev * ct_prev + s_loc * upd
        m_scr[h] = jnp.broadcast_to(m_new, m_scr.shape[1:])


def _mlstm(qk, v, o, small, conv_w, conv_b, gbias, bsz):
    t = qk.shape[0]
    seq = t // bsz
    n = min(SEQ_CHUNK, seq)
    nc = seq // n
    row = lambda w: pl.BlockSpec((n, w), lambda b, c: (b * nc + c, 0))
    tail_blocks = n // TAIL
    return pl.pallas_call(
        _mlstm_kernel,
        grid=(bsz, nc),
        in_specs=[
            row(2 * ML_WIDTH),
            pl.BlockSpec((TAIL, 2 * ML_WIDTH), lambda b, c: (jnp.maximum((b * nc + c) * tail_blocks - 1, 0), 0)),
            row(ML_WIDTH), row(ML_WIDTH), row(LANES),
            _const_spec((CONV_K, 2 * ML_WIDTH)), _const_spec((1, 2 * ML_WIDTH)), _const_spec((1, LANES)),
        ],
        out_specs=row(ML_WIDTH),
        out_shape=jax.ShapeDtypeStruct((t, ML_WIDTH), BF16),
        scratch_shapes=[
            pltpu.VMEM((TAIL + n, 2 * ML_WIDTH), F32),
            pltpu.VMEM((ML_HEADS, ML_HEAD_DIM, 2 * ML_HEAD_DIM), F32),
            pltpu.VMEM((ML_HEADS, 1, LANES), F32),
        ],
        compiler_params=_cparams("arbitrary", "arbitrary"),
        name="mlstm",
    )(qk, qk, v, o, small, conv_w, conv_b.reshape(1, -1), gbias)


def _ssd_kernel(z_ref, xbc_ref, tail_ref, small_ref, cw_ref, cb_ref, gbias_ref, alog_ref, dskip_ref, normw_ref, expand_ref,
                y_ref, pad_scr, state_scr, ydiag_scr):
    first = pl.program_id(1) == 0
    n = z_ref.shape[0]
    gw = M2_GROUP_WIDTH
    hd = M2_HEAD_DIM

    @pl.when(first)
    def _():
        state_scr[...] = jnp.zeros_like(state_scr)

    xbc = _silu(_causal_conv(xbc_ref, tail_ref, cw_ref, cb_ref, pad_scr, first))
    xs = xbc[:, :M2_WIDTH]

    lane = lax.broadcasted_iota(jnp.int32, (1, LANES), 1)
    dt_lanes = (lane >= SLOT_DT) & (lane < SLOT_DT + M2_HEADS)
    dt = jnp.where(dt_lanes, jax.nn.softplus(small_ref[...] + gbias_ref[...]), 0.0)
    a_row = jnp.where(dt_lanes, -jnp.exp(alog_ref[...]), 0.0)
    tri = _lower_tri(n)
    cum = _dot3_left(tri.astype(BF16), dt * a_row)
    cum_t = cum.T
    expand = expand_ref[...]
    dt_e = _dot3_right(dt, expand)
    cum_e = _dot3_right(cum, expand)
    cum_end_e = cum_e[n - 1:n, :]
    xdt = xs * dt_e
    x_in = (xdt * jnp.exp(cum_end_e - cum_e)).astype(BF16)
    decay_in = jnp.exp(cum_e)
    decay_chunk = jnp.exp(cum_end_e)
    xdt_b = xdt.astype(BF16)

    for g in range(M2_GROUPS):
        b_g = xbc[:, M2_WIDTH + g * M2_STATE:M2_WIDTH + (g + 1) * M2_STATE]
        c_g = xbc[:, M2_WIDTH + (M2_GROUPS + g) * M2_STATE:M2_WIDTH + (M2_GROUPS + g + 1) * M2_STATE].astype(BF16)
        b_t = b_g.T.astype(BF16)
        cb = _dot(c_g, b_t)
        for r in range(M2_HPG):
            head = g * M2_HPG + r
            cum_col = cum[:, SLOT_DT + head:SLOT_DT + head + 1]
            cum_row = cum_t[SLOT_DT + head:SLOT_DT + head + 1, :]
            w = cb * jnp.exp(jnp.where(tri, cum_col - cum_row, -jnp.inf))
            ydiag_scr[:, head * hd:(head + 1) * hd] = _dot(w.astype(BF16), xdt_b[:, head * hd:(head + 1) * hd])
        cols = slice(g * gw, (g + 1) * gw)
        st = state_scr[g]
        y_off = _dot(c_g, st.astype(BF16)) * decay_in[:, cols]
        y = ydiag_scr[:, cols] + y_off + dskip_ref[:, cols] * xs[:, cols]
        y = y * _silu(z_ref[:, cols].astype(F32))
        y = y * lax.rsqrt(jnp.mean(y * y, axis=-1, keepdims=True) + EPS)
        y_ref[:, cols] = (y * normw_ref[:, cols]).astype(y_ref.dtype)
        state_scr[g] = st * decay_chunk[:, cols] + _dot(b_t, x_in[:, cols])


def _ssd(z, xbc, small, conv_w, conv_b, gbias, alog_row, dskip_row, norm_w, bsz):
    t = z.shape[0]
    seq = t // bsz
    n = min(SEQ_CHUNK, seq)
    nc = seq // n
    row = lambda w: pl.BlockSpec((n, w), lambda b, c: (b * nc + c, 0))
    tail_blocks = n // TAIL
    expand = np.zeros((LANES, M2_WIDTH), np.float32)
    for h in range(M2_HEADS):
        expand[SLOT_DT + h, h * M2_HEAD_DIM:(h + 1) * M2_HEAD_DIM] = 1.0
    return pl.pallas_call(
        _ssd_kernel,
        grid=(bsz, nc),
        in_specs=[
            row(M2_WIDTH), row(M2_CONV_DIM),
            pl.BlockSpec((TAIL, M2_CONV_DIM), lambda b, c: (jnp.maximum((b * nc + c) * tail_blocks - 1, 0), 0)),
            row(LANES),
            _const_spec((CONV_K, M2_CONV_DIM)), _const_spec((1, M2_CONV_DIM)), _const_spec((1, LANES)), _const_spec((1, LANES)),
            _const_spec((1, M2_WIDTH)), _const_spec((1, M2_WIDTH)), _const_spec((LANES, M2_WIDTH)),
        ],
        out_specs=row(M2_WIDTH),
        out_shape=jax.ShapeDtypeStruct((t, M2_WIDTH), BF16),
        scratch_shapes=[
            pltpu.VMEM((TAIL + n, M2_CONV_DIM), F32),
            pltpu.VMEM((M2_GROUPS, M2_STATE, M2_GROUP_WIDTH), F32),
            pltpu.VMEM((n, M2_WIDTH), F32),
        ],
        compiler_params=_cparams("arbitrary", "arbitrary"),
        name="ssd",
    )(z, xbc, xbc, small, conv_w, conv_b.reshape(1, -1), gbias, alog_row, dskip_row, norm_w.reshape(1, -1), jnp.asarray(expand, BF16))


def _merge_kernel(x_ref, yg_ref, yb_ref, yc_ref, g_ref, wglu_ref, wa_ref, wb_ref, wc_ref, wout_ref, nmp_ref, out_ref, ys_scr):
    d = D_MODEL
    nchunks = yg_ref.shape[1]
    for jb in range(S5_GROUPS // GROUPS_PER_VREG):
        for half in range(S5_CHUNK // GROUPS_PER_VREG):
            v = [yg_ref[jb * GROUPS_PER_VREG + g, :, half * LANES:(half + 1) * LANES].astype(F32) for g in range(GROUPS_PER_VREG)]
            v = _block_transpose(v)
            for t in range(GROUPS_PER_VREG):
                ys_scr[jb, pl.ds(half * GROUPS_PER_VREG + t, nchunks, stride=S5_CHUNK), :] = v[t]
    ys5 = jnp.concatenate([ys_scr[jb] for jb in range(S5_WIDTH // LANES)], axis=1)
    ya = ys5 * jax.nn.sigmoid(_dot(ys5.astype(BF16), wglu_ref[...]))
    gate = lambda i: jax.nn.sigmoid(g_ref[:, i * d:(i + 1) * d].astype(F32))
    merged = (gate(0) * _dot(ya.astype(BF16), wa_ref[...]) + gate(1) * _dot(yb_ref[...], wb_ref[...])
              + gate(2) * _dot(yc_ref[...], wc_ref[...]))
    mix = _dot(merged.astype(BF16), wout_ref[...])
    out_ref[...] = x_ref[...] + _rms(mix, nmp_ref[...])


def _merge(x2, yg, yb, yc, gates, weights, norm_post, layer):
    t = x2.shape[0]
    tm = min(TOKEN_BLOCK, t)
    row = lambda w: pl.BlockSpec((tm, w), lambda i: (i, 0))
    return pl.pallas_call(
        _merge_kernel,
        grid=(t // tm,),
        in_specs=[row(D_MODEL), pl.BlockSpec((S5_GROUPS, tm // S5_CHUNK, S5_K), lambda i: (0, i, 0)),
                  row(ML_WIDTH), row(M2_WIDTH), row(N_BRANCH * D_MODEL)]
        + [_layer_spec(w, layer) for w in weights] + [_const_spec((1, D_MODEL))],
        out_specs=row(D_MODEL),
        out_shape=jax.ShapeDtypeStruct((t, D_MODEL), F32),
        scratch_shapes=[pltpu.VMEM((S5_WIDTH // LANES, tm, LANES), F32)],
        compiler_params=_cparams("parallel"),
        name="merge",
    )(x2, yg, yb, yc, gates, *weights, norm_post.reshape(1, D_MODEL))


def _ffn_kernel(x_ref, npre_ref, npost_ref, wg_ref, wu_ref, wd_ref, out_ref):
    x = x_ref[...]
    hb = _rms(x, npre_ref[...]).astype(BF16)
    act = _silu(_dot(hb, wg_ref[...])) * _dot(hb, wu_ref[...])
    ffn = _dot(act.astype(BF16), wd_ref[...])
    out_ref[...] = x + _rms(ffn, npost_ref[...])


def _ffn(x2, weights, norm_pre, norm_post, layer):
    t = x2.shape[0]
    tm = min(TOKEN_BLOCK, t)
    row = pl.BlockSpec((tm, D_MODEL), lambda i: (i, 0))
    return pl.pallas_call(
        _ffn_kernel,
        grid=(t // tm,),
        in_specs=[row, _const_spec((1, D_MODEL)), _const_spec((1, D_MODEL))] + [_layer_spec(w, layer) for w in weights],
        out_specs=row,
        out_shape=jax.ShapeDtypeStruct((t, D_MODEL), F32),
        compiler_params=_cparams("parallel"),
        name="ffn",
    )(x2, norm_pre.reshape(1, D_MODEL), norm_post.reshape(1, D_MODEL), *weights)


def _gate_bias_row(bias_i, bias_f, dt_bias):
    pad = jnp.zeros((LANES - 2 * ML_HEADS - M2_HEADS,), F32)
    return jnp.concatenate([bias_i, bias_f, dt_bias, pad]).reshape(1, LANES)


def _head_lane_row(v):
    pad_l = jnp.zeros((SLOT_DT,), F32)
    pad_r = jnp.zeros((LANES - SLOT_DT - M2_HEADS,), F32)
    return jnp.concatenate([pad_l, v, pad_r]).reshape(1, LANES)


def _layer(x2, bsz, p, big, layer):
    ug, qk, v, o, z, xbc, gates, small = _in_proj(x2, p["norm_mix_pre"], big["w_in"], layer)
    mats = _s5_prep(p["s5_a_re"], p["s5_a_im"], p["s5_b_re"], p["s5_b_im"], p["s5_c_re"], p["s5_c_im"], p["s5_d"], p["s5_log_dt"])
    yg = _s5_mixer(ug, bsz, mats)
    gbias = _gate_bias_row(p["ml_bias_i"], p["ml_bias_f"], p["m2_dt_bias"])
    yb = _mlstm(qk, v, o, small, p["ml_conv_w"], p["ml_conv_b"], gbias, bsz)
    yc = _ssd(z, xbc, small, p["m2_conv_w"], p["m2_conv_b"], gbias, _head_lane_row(p["m2_a_log"]),
              jnp.repeat(p["m2_d"], M2_HEAD_DIM).reshape(1, M2_WIDTH), p["m2_norm_w"], bsz)
    x1 = _merge(x2, yg, yb, yc, gates, big["merge"], p["norm_mix_post"], layer)
    return _ffn(x1, big["ffn"], p["norm_ffn_pre"], p["norm_ffn_post"], layer)


_PARAM_NAMES = ("norm_mix_pre", "norm_mix_post", "w_in", "s5_a_re", "s5_a_im", "s5_b_re", "s5_b_im", "s5_c_re", "s5_c_im", "s5_d",
                "s5_log_dt", "s5_w_glu", "ml_conv_w", "ml_conv_b", "ml_bias_i", "ml_bias_f", "m2_conv_w", "m2_conv_b", "m2_dt_bias",
                "m2_a_log", "m2_d", "m2_norm_w", "w_br_a", "w_br_b", "w_br_c", "w_out", "norm_ffn_pre", "norm_ffn_post",
                "w_ffn_gate", "w_ffn_up", "w_ffn_down")


def kernel(x, norm_mix_pre, norm_mix_post, w_in, s5_a_re, s5_a_im, s5_b_re, s5_b_im, s5_c_re, s5_c_im, s5_d, s5_log_dt, s5_w_glu, ml_conv_w, ml_conv_b, ml_bias_i, ml_bias_f, m2_conv_w, m2_conv_b, m2_dt_bias, m2_a_log, m2_d, m2_norm_w, w_br_a, w_br_b, w_br_c, w_out, norm_ffn_pre, norm_ffn_post, w_ffn_gate, w_ffn_up, w_ffn_down):
    stacked = (norm_mix_pre, norm_mix_post, w_in, s5_a_re, s5_a_im, s5_b_re, s5_b_im, s5_c_re, s5_c_im, s5_d, s5_log_dt, s5_w_glu,
               ml_conv_w, ml_conv_b, ml_bias_i, ml_bias_f, m2_conv_w, m2_conv_b, m2_dt_bias, m2_a_log, m2_d, m2_norm_w,
               w_br_a, w_br_b, w_br_c, w_out, norm_ffn_pre, norm_ffn_post, w_ffn_gate, w_ffn_up, w_ffn_down)
    bsz, seq, d = x.shape
    x2 = x.reshape(bsz * seq, d)
    big = {
        "w_in": _split_w_in(w_in),
        "merge": [w.astype(BF16) for w in (s5_w_glu, w_br_a, w_br_b, w_br_c, w_out)],
        "ffn": [w.astype(BF16) for w in (w_ffn_gate, w_ffn_up, w_ffn_down)],
    }
    for layer in range(norm_mix_pre.shape[0]):
        x2 = _layer(x2, bsz, {name: arr[layer] for name, arr in zip(_PARAM_NAMES, stacked)}, big, layer)
    return x2.reshape(bsz, seq, d)
```

```python
import jax
import jax.numpy as jnp
import numpy as np
from jax import lax
from jax.experimental import pallas as pl
from jax.experimental.pallas import tpu as pltpu

F32 = jnp.float32
BF16 = jnp.bfloat16

D_MODEL = 1024
EPS = 1e-6
CONV_K = 4

S5_WIDTH = 512
S5_GROUP = 16
S5_GROUPS = S5_WIDTH // S5_GROUP
S5_STATE = 64
S5_CHUNK = 16
S5_K = S5_CHUNK * S5_GROUP
S5_CHUNK_BLOCK = 256

ML_WIDTH = 512
ML_HEADS = 4
ML_HEAD_DIM = ML_WIDTH // ML_HEADS

M2_WIDTH = 1024
M2_HEAD_DIM = 64
M2_HEADS = M2_WIDTH // M2_HEAD_DIM
M2_GROUPS = 2
M2_HPG = M2_HEADS // M2_GROUPS
M2_STATE = 128
M2_CONV_DIM = M2_WIDTH + 2 * M2_GROUPS * M2_STATE
M2_GROUP_WIDTH = M2_WIDTH // M2_GROUPS

N_BRANCH = 3
FFN_HIDDEN = -(-(8 * D_MODEL) // (3 * 256)) * 256

LANES = 128
SUBLANES = 8
TAIL = SUBLANES
GROUPS_PER_VREG = LANES // S5_GROUP

SLOT_I = 0
SLOT_F = ML_HEADS
SLOT_DT = 2 * ML_HEADS

SEQ_CHUNK = 128
TOKEN_BLOCK = 512

VMEM_LIMIT = 56 * 1024 * 1024


def _cparams(*sem):
    return pltpu.CompilerParams(dimension_semantics=sem, vmem_limit_bytes=VMEM_LIMIT)


def _const_spec(shape):
    nd = len(shape)
    return pl.BlockSpec(shape, lambda *_: (0,) * nd, pipeline_mode=pl.Buffered(1))


def _layer_spec(stacked, layer):
    shape = stacked.shape[1:]
    zeros = (0,) * len(shape)
    return pl.BlockSpec((None,) + shape, lambda *_: (layer,) + zeros, pipeline_mode=pl.Buffered(1))


def _rms(x, w):
    return x * lax.rsqrt(jnp.mean(x * x, axis=-1, keepdims=True) + EPS) * w


def _split3(x):
    hi = x.astype(BF16)
    r1 = x - hi.astype(F32)
    mid = r1.astype(BF16)
    lo = (r1 - mid.astype(F32)).astype(BF16)
    return hi, mid, lo


def _dot(a, b):
    return jnp.dot(a, b, preferred_element_type=F32)


def _dot_nt(a, b):
    return lax.dot_general(a, b, (((1,), (1,)), ((), ())), preferred_element_type=F32)


def _dot3_left(m_bf16, x):
    hi, mid, lo = _split3(x)
    return _dot(m_bf16, hi) + _dot(m_bf16, mid) + _dot(m_bf16, lo)


def _dot3_right(x, m_bf16):
    hi, mid, lo = _split3(x)
    return _dot(hi, m_bf16) + _dot(mid, m_bf16) + _dot(lo, m_bf16)


def _lower_tri(n):
    r = lax.broadcasted_iota(jnp.int32, (n, n), 0)
    c = lax.broadcasted_iota(jnp.int32, (n, n), 1)
    return r >= c


def _block_transpose(v):
    lane_blk = lax.broadcasted_iota(jnp.int32, v[0].shape, 1) // S5_GROUP
    b = GROUPS_PER_VREG // 2
    while b:
        bit_set = (lane_blk & b) != 0
        out = list(v)
        for r in range(GROUPS_PER_VREG):
            if r & b == 0:
                rp = r | b
                out[r] = jnp.where(bit_set, pltpu.roll(v[rp], b * S5_GROUP, 1), v[r])
                out[rp] = jnp.where(bit_set, v[rp], pltpu.roll(v[r], LANES - b * S5_GROUP, 1))
        v = out
        b //= 2
    return v


IN_SEGMENTS = (
    ("qk", 2 * ML_WIDTH, BF16),
    ("v", ML_WIDTH, BF16),
    ("o", ML_WIDTH, BF16),
    ("z", M2_WIDTH, BF16),
    ("xbc", M2_CONV_DIM, BF16),
    ("gates", N_BRANCH * D_MODEL, BF16),
    ("small", LANES, F32),
)


def _split_w_in(w_in):
    splits = (S5_WIDTH, ML_WIDTH, ML_WIDTH, ML_WIDTH, ML_WIDTH, ML_HEADS, ML_HEADS, M2_WIDTH, M2_CONV_DIM, M2_HEADS, N_BRANCH * D_MODEL)
    offs = np.cumsum((0,) + splits)
    col = lambda n0, n1: w_in[..., offs[n0]:offs[n1]].astype(BF16)
    pad = jnp.zeros(w_in.shape[:-1] + (LANES - 2 * ML_HEADS - M2_HEADS,), BF16)
    small = jnp.concatenate([col(5, 7), col(9, 10), pad], axis=-1)
    return [col(0, 5), col(7, 8), col(8, 9), col(10, 11), small]


def _silu(x):
    return x * jax.nn.sigmoid(x)


def _causal_conv(x_ref, tail_ref, w_ref, b_ref, pad_scr, first):
    n = x_ref.shape[0]
    tail = tail_ref[...].astype(F32)
    pad_scr[0:TAIL, :] = jnp.where(first, 0.0, tail)
    pad_scr[TAIL:TAIL + n, :] = x_ref[...].astype(F32)
    acc = b_ref[...] + w_ref[CONV_K - 1:CONV_K, :] * pad_scr[TAIL:TAIL + n, :]
    for j in range(CONV_K - 1):
        acc = acc + w_ref[j:j + 1, :] * pad_scr[pl.ds(TAIL - (CONV_K - 1) + j, n), :]
    return acc


def _in_proj_kernel(x_ref, nw_ref, wa_ref, wz_ref, wxbc_ref, wg_ref, wsmall_ref,
                    ug_ref, qk_ref, v_ref, o_ref, z_ref, xbc_ref, g_ref, small_ref, u_scr):
    hb = _rms(x_ref[...], nw_ref[...]).astype(BF16)
    off = S5_WIDTH
    for ref in (qk_ref, v_ref, o_ref):
        width = ref.shape[1]
        ref[...] = _dot(hb, wa_ref[:, off:off + width]).astype(ref.dtype)
        off += width
    for ref, w_ref in ((z_ref, wz_ref), (xbc_ref, wxbc_ref), (g_ref, wg_ref), (small_ref, wsmall_ref)):
        ref[...] = _dot(hb, w_ref[...]).astype(ref.dtype)

    u = _dot(hb, wa_ref[:, 0:S5_WIDTH])
    nchunks = u_scr.shape[1] // S5_CHUNK
    for jb in range(S5_GROUPS // GROUPS_PER_VREG):
        u_scr[jb] = u[:, jb * LANES:(jb + 1) * LANES]
        for half in range(S5_CHUNK // GROUPS_PER_VREG):
            v = [u_scr[jb, pl.ds(half * GROUPS_PER_VREG + t, nchunks, stride=S5_CHUNK), :] for t in range(GROUPS_PER_VREG)]
            v = _block_transpose(v)
            for g in range(GROUPS_PER_VREG):
                ug_ref[jb * GROUPS_PER_VREG + g, :, half * LANES:(half + 1) * LANES] = v[g].astype(ug_ref.dtype)


def _in_proj(x2, norm_w, w_parts, layer):
    t = x2.shape[0]
    tm = min(TOKEN_BLOCK, t)
    nchunks = tm // S5_CHUNK
    return pl.pallas_call(
        _in_proj_kernel,
        grid=(t // tm,),
        in_specs=[pl.BlockSpec((tm, D_MODEL), lambda i: (i, 0)), _const_spec((1, D_MODEL))]
        + [_layer_spec(w, layer) for w in w_parts],
        out_specs=[pl.BlockSpec((S5_GROUPS, nchunks, S5_K), lambda i: (0, i, 0))]
        + [pl.BlockSpec((tm, w), lambda i: (i, 0)) for _, w, _ in IN_SEGMENTS],
        out_shape=[jax.ShapeDtypeStruct((S5_GROUPS, t // S5_CHUNK, S5_K), BF16)]
        + [jax.ShapeDtypeStruct((t, w), dt) for _, w, dt in IN_SEGMENTS],
        scratch_shapes=[pltpu.VMEM((S5_WIDTH // LANES, tm, LANES), F32)],
        compiler_params=_cparams("parallel"),
        name="in_proj",
    )(x2, norm_w.reshape(1, D_MODEL), *w_parts)


def _s5_prep_kernel(lr_ref, li_ref, logdt_ref, bt_ref, bti_ref, btr_t_ref, bti_t_ref, cr_ref, ci_ref, d_ref,
                    tt_ref, w_ref, mt_ref, pq_ref):
    p64 = S5_STATE
    lr, li = lr_ref[0], li_ref[0]
    dt = jnp.exp(logdt_ref[0])
    mag = jnp.exp(lr * dt)
    lb_re = mag * jnp.cos(li * dt)
    lb_im = mag * jnp.sin(li * dt)
    den = lr * lr + li * li
    f_re = ((lb_re - 1.0) * lr + lb_im * li) / den
    f_im = (lb_im * lr - (lb_re - 1.0) * li) / den

    pw = [(jnp.ones_like(lb_re), jnp.zeros_like(lb_im))]
    for _ in range(S5_CHUNK):
        pr, pi = pw[-1]
        pw.append((pr * lb_re - pi * lb_im, pr * lb_im + pi * lb_re))

    cr, ci = cr_ref[0], ci_ref[0]
    btr_t, bti_t = btr_t_ref[0], bti_t_ref[0]
    g_re, g_im, m_re, m_im, w_re, w_im = [], [], [], [], [], []
    for k in range(S5_CHUNK):
        pr, pi = pw[k]
        afr = pr * f_re - pi * f_im
        afi = pr * f_im + pi * f_re
        g_re.append(afr * cr - afi * ci)
        g_im.append(afr * ci + afi * cr)
        qr, qi = pw[k + 1]
        m_re.append(qr * cr - qi * ci)
        m_im.append(qr * ci + qi * cr)
        sr, si = pw[S5_CHUNK - 1 - k]
        wfr = sr * f_re - si * f_im
        wfi = sr * f_im + si * f_re
        w_re.append(wfr * btr_t - wfi * bti_t)
        w_im.append(wfr * bti_t + wfi * btr_t)
    cat = lambda blocks: jnp.concatenate(blocks, axis=0)
    w_ref[0, :, 0:p64] = cat(w_re).astype(w_ref.dtype)
    w_ref[0, :, p64:2 * p64] = cat(w_im).astype(w_ref.dtype)
    mt_ref[0, :, 0:p64] = cat(m_re).astype(mt_ref.dtype)
    mt_ref[0, :, p64:2 * p64] = (-cat(m_im)).astype(mt_ref.dtype)

    hi = lax.Precision.HIGHEST
    kk = (jnp.dot(cat(g_re), bt_ref[0], precision=hi, preferred_element_type=F32)
          - jnp.dot(cat(g_im), bti_ref[0], precision=hi, preferred_element_type=F32))
    rows = lax.broadcasted_iota(jnp.int32, (S5_K, S5_K), 0)
    cols = lax.broadcasted_iota(jnp.int32, (S5_K, S5_K), 1)
    col_blk = cols // S5_GROUP
    tt = jnp.where(rows == cols, d_ref[0], 0.0)
    for s in range(S5_CHUNK):
        if s == 0:
            shifted = kk
        else:
            shifted = jnp.concatenate([jnp.zeros((s * S5_GROUP, S5_K), F32), kk[:S5_K - s * S5_GROUP, :]], axis=0)
        tt = tt + jnp.where(col_blk == s, shifted, 0.0)
    tt_ref[0] = tt.astype(tt_ref.dtype)

    cr16, ci16 = pw[S5_CHUNK]
    pq_ref[0, 0:1, :] = jnp.concatenate([cr16, cr16], axis=1)
    pq_ref[0, 1:2, :] = jnp.concatenate([-ci16, ci16], axis=1)


def _s5_prep(a_re, a_im, b_re, b_im, c_re, c_im, d_skip, log_dt):
    g, p = S5_GROUPS, S5_STATE
    tile_lanes = lambda b: jnp.tile(b, (1, 1, S5_CHUNK))
    swap = lambda b: jnp.transpose(b, (0, 2, 1))
    args = (
        a_re.reshape(g, 1, p), a_im.reshape(g, 1, p), log_dt.reshape(g, 1, 1),
        tile_lanes(b_re), tile_lanes(b_im), swap(b_re), swap(b_im), c_re, c_im,
        jnp.tile(d_skip, (1, S5_CHUNK)).reshape(g, 1, S5_K),
    )
    spec = lambda a: pl.BlockSpec((1,) + a.shape[1:], lambda i: (i, 0, 0))
    out_shapes = (
        jax.ShapeDtypeStruct((g, S5_K, S5_K), BF16),
        jax.ShapeDtypeStruct((g, S5_K, 2 * p), BF16),
        jax.ShapeDtypeStruct((g, S5_K, 2 * p), BF16),
        jax.ShapeDtypeStruct((g, 2, 2 * p), F32),
    )
    return pl.pallas_call(
        _s5_prep_kernel,
        grid=(g,),
        in_specs=[spec(a) for a in args],
        out_specs=[spec(o) for o in out_shapes],
        out_shape=out_shapes,
        compiler_params=_cparams("parallel"),
        name="s5_prep",
    )(*args)


def _s5_kernel(ug_ref, tt_ref, w_ref, mt_ref, pq_ref, yg_ref, l_scr, s_scr, state_scr):
    ng = S5_GROUPS
    nchunks = ug_ref.shape[1]

    @pl.when(pl.program_id(1) == 0)
    def _():
        state_scr[...] = jnp.zeros_like(state_scr)

    def local(g, carry):
        l_scr[pl.ds(g, nchunks, stride=ng), :] = _dot(ug_ref[g], w_ref[g])
        return carry

    lax.fori_loop(0, ng, local, 0, unroll=2)

    p_mul = pq_ref[0]
    q_mul = pq_ref[1]

    def step(j, carry):
        x, xs = carry
        row = pl.multiple_of(j * ng, ng)
        s_scr[pl.ds(row, ng), :] = x
        l = l_scr[pl.ds(row, ng), :]
        ls = pltpu.roll(l, S5_STATE, 1)
        return p_mul * x + q_mul * xs + l, p_mul * xs - q_mul * x + ls

    x, xs = lax.fori_loop(0, nchunks, step, (state_scr[0], state_scr[1]), unroll=8)
    state_scr[0] = x
    state_scr[1] = xs

    def output(g, carry):
        u = ug_ref[g]
        s_in = s_scr[pl.ds(g, nchunks, stride=ng), :].astype(BF16)
        y = _dot_nt(u, tt_ref[g]) + _dot_nt(s_in, mt_ref[g])
        yg_ref[g] = jax.nn.gelu(y).astype(yg_ref.dtype)
        return carry

    lax.fori_loop(0, ng, output, 0, unroll=2)


def _s5_mixer(ug, bsz, mats):
    tt, w_mat, mt, pq = mats
    pq = jnp.transpose(pq, (1, 0, 2))
    g, nch, _ = ug.shape
    nch_b = nch // bsz
    cb = min(S5_CHUNK_BLOCK, nch_b)
    nblk = nch_b // cb
    blk = pl.BlockSpec((g, cb, S5_K), lambda b, j: (0, b * nblk + j, 0))
    return pl.pallas_call(
        _s5_kernel,
        grid=(bsz, nblk),
        in_specs=[blk, _const_spec(tt.shape), _const_spec(w_mat.shape), _const_spec(mt.shape), _const_spec(pq.shape)],
        out_specs=blk,
        out_shape=jax.ShapeDtypeStruct(ug.shape, BF16),
        scratch_shapes=[
            pltpu.VMEM((cb * g, 2 * S5_STATE), F32),
            pltpu.VMEM((cb * g, 2 * S5_STATE), F32),
            pltpu.VMEM((2, g, 2 * S5_STATE), F32),
        ],
        compiler_params=_cparams("arbitrary", "arbitrary"),
        name="s5",
    )(ug, tt, w_mat, mt, pq)


def _mlstm_kernel(qk_ref, tail_ref, v_ref, o_ref, small_ref, cw_ref, cb_ref, gbias_ref, y_ref, pad_scr, ct_scr, m_scr):
    first = pl.program_id(1) == 0
    n = qk_ref.shape[0]
    dh = ML_HEAD_DIM

    @pl.when(first)
    def _():
        ct_scr[...] = jnp.zeros_like(ct_scr)
        m_scr[...] = jnp.zeros_like(m_scr)

    ct_in = [ct_scr[h] for h in range(ML_HEADS)]
    m_in = [m_scr[h][:, 0:1] for h in range(ML_HEADS)]
    qk = _silu(_causal_conv(qk_ref, tail_ref, cw_ref, cb_ref, pad_scr, first))
    gate = small_ref[...] + gbias_ref[...]
    lf = jax.nn.log_sigmoid(gate)
    tri = _lower_tri(n)
    b_all = _dot3_left(tri.astype(BF16), lf)
    gate_t = gate.T
    b_t = b_all.T
    ones_col = (lax.broadcasted_iota(jnp.int32, (n, dh), 1) == 0).astype(BF16)

    outs, ct_out, m_out = [], [], []
    for h in range(ML_HEADS):
        q = qk[:, h * dh:(h + 1) * dh].astype(BF16)
        k = qk[:, ML_WIDTH + h * dh:ML_WIDTH + (h + 1) * dh] * (dh ** -0.5)
        k_t = k.T.astype(BF16)
        v_ext = jnp.concatenate([v_ref[:, h * dh:(h + 1) * dh], ones_col], axis=1)
        b_col = b_all[:, SLOT_F + h:SLOT_F + h + 1]
        ig_col = gate[:, SLOT_I + h:SLOT_I + h + 1]
        b_row = b_t[SLOT_F + h:SLOT_F + h + 1, :]
        ig_row = gate_t[SLOT_I + h:SLOT_I + h + 1, :]
        m_prev = m_in[h]
        ct_prev = ct_in[h]

        dmat = jnp.where(tri, b_col - b_row + ig_row, -jnp.inf)
        inter = b_col + m_prev
        m_t = jnp.maximum(jnp.max(dmat, axis=1, keepdims=True), inter)
        p = _dot(q, k_t) * jnp.exp(dmat - m_t)
        w_inter = jnp.exp(inter - m_t)
        num_den = _dot(p.astype(BF16), v_ext) + w_inter * _dot(q, ct_prev.astype(BF16))
        den = num_den[:, dh:dh + 1]
        denom = jnp.maximum(jnp.abs(den), jnp.exp(-m_t))
        out = num_den[:, :dh] / denom * jax.nn.sigmoid(o_ref[:, h * dh:(h + 1) * dh].astype(F32))
        outs.append(out.astype(y_ref.dtype))

        b_end = b_col[n - 1:n, :]
        a_end = b_end - b_col + ig_col
        m_loc = jnp.max(a_end, axis=0, keepdims=True)
        w_end = jnp.exp(a_end - m_loc)
        m_new = jnp.maximum(b_end + m_prev, m_loc)
        s_prev = jnp.exp(b_end + m_prev - m_new)
        s_loc = jnp.exp(m_loc - m_new)
        upd = _dot(k_t, (w_end * v_ext.astype(F32)).astype(BF16))
        ct_out.append(s_prev * ct_prev + s_loc * upd)
        m_out.append(jnp.broadcast_to(m_new, m_scr.shape[1:]))

    y_ref[...] = jnp.concatenate(outs, axis=1)
    for h in range(ML_HEADS):
        ct_scr[h] = ct_out[h]
        m_scr[h] = m_out[h]


def _seq_specs(n, nc):
    row = lambda w: pl.BlockSpec((n, w), lambda b, c: (b * nc + c, 0))
    tail = lambda w: pl.BlockSpec((TAIL, w), lambda b, c: (jnp.maximum((b * nc + c) * (n // TAIL) - 1, 0), 0))
    return row, tail


def _mlstm_call_parts(qk, v, o, small, conv_w, conv_b, gbias, n, nc):
    row, tail = _seq_specs(n, nc)
    in_specs = [row(2 * ML_WIDTH), tail(2 * ML_WIDTH), row(ML_WIDTH), row(ML_WIDTH), row(LANES),
                _const_spec((CONV_K, 2 * ML_WIDTH)), _const_spec((1, 2 * ML_WIDTH)), _const_spec((1, LANES))]
    args = (qk, qk, v, o, small, conv_w, conv_b.reshape(1, -1), gbias)
    scratch = [pltpu.VMEM((TAIL + n, 2 * ML_WIDTH), F32),
               pltpu.VMEM((ML_HEADS, ML_HEAD_DIM, 2 * ML_HEAD_DIM), F32),
               pltpu.VMEM((ML_HEADS, 1, LANES), F32)]
    return in_specs, args, row(ML_WIDTH), jax.ShapeDtypeStruct((qk.shape[0], ML_WIDTH), BF16), scratch


def _ssd_kernel(z_ref, xbc_ref, tail_ref, small_ref, cw_ref, cb_ref, gbias_ref, alog_ref, dskip_ref, normw_ref, expand_ref,
                y_ref, pad_scr, state_scr):
    first = pl.program_id(1) == 0
    n = z_ref.shape[0]
    gw = M2_GROUP_WIDTH
    hd = M2_HEAD_DIM

    @pl.when(first)
    def _():
        state_scr[...] = jnp.zeros_like(state_scr)

    state_in = [state_scr[g] for g in range(M2_GROUPS)]
    xbc = _silu(_causal_conv(xbc_ref, tail_ref, cw_ref, cb_ref, pad_scr, first))
    xs = xbc[:, :M2_WIDTH]
    lane = lax.broadcasted_iota(jnp.int32, (1, LANES), 1)
    dt_lanes = (lane >= SLOT_DT) & (lane < SLOT_DT + M2_HEADS)
    dt = jnp.where(dt_lanes, jax.nn.softplus(small_ref[...] + gbias_ref[...]), 0.0)
    a_row = jnp.where(dt_lanes, -jnp.exp(alog_ref[...]), 0.0)
    tri = _lower_tri(n)
    cum = _dot3_left(tri.astype(BF16), dt * a_row)
    cum_t = cum.T
    cum_end = cum[n - 1:n, :]
    expand = expand_ref[...]
    dt_e = _dot(dt.astype(BF16), expand)
    decay_in = _dot(jnp.exp(cum).astype(BF16), expand)
    decay_out = _dot((dt * jnp.exp(cum_end - cum)).astype(BF16), expand)
    decay_chunk = _dot3_right(jnp.exp(cum[n - SUBLANES:n, :]), expand)[SUBLANES - 1:SUBLANES, :]
    xdt_b = (xs * dt_e).astype(BF16)
    x_in = (xs * decay_out).astype(BF16)
    left_half = lax.broadcasted_iota(jnp.int32, (n, 2 * hd), 1) < hd

    y_out, state_out = [], []
    for g in range(M2_GROUPS):
        b_g = xbc[:, M2_WIDTH + g * M2_STATE:M2_WIDTH + (g + 1) * M2_STATE]
        c_g = xbc[:, M2_WIDTH + (M2_GROUPS + g) * M2_STATE:M2_WIDTH + (M2_GROUPS + g + 1) * M2_STATE].astype(BF16)
        b_t = b_g.T.astype(BF16)
        cb = _dot(c_g, b_t)
        y_diag = []
        for pair in range(M2_HPG // 2):
            head = g * M2_HPG + 2 * pair
            ws = []
            for hh in (head, head + 1):
                cum_col = cum[:, SLOT_DT + hh:SLOT_DT + hh + 1]
                cum_row = cum_t[SLOT_DT + hh:SLOT_DT + hh + 1, :]
                ws.append((cb * jnp.exp(jnp.where(tri, cum_col - cum_row, -jnp.inf))).astype(BF16))
            xp = xdt_b[:, head * hd:(head + 2) * hd]
            zero = jnp.zeros_like(xp)
            rhs = jnp.concatenate([jnp.where(left_half, xp, zero), jnp.where(left_half, zero, xp)], axis=0)
            y_diag.append(_dot(jnp.concatenate(ws, axis=1), rhs))
        cols = slice(g * gw, (g + 1) * gw)
        st = state_in[g]
        y_off = _dot(c_g, st.astype(BF16)) * decay_in[:, cols]
        y = jnp.concatenate(y_diag, axis=1) + y_off + dskip_ref[:, cols] * xs[:, cols]
        y = y * _silu(z_ref[:, cols].astype(F32))
        y = y * lax.rsqrt(jnp.mean(y * y, axis=-1, keepdims=True) + EPS)
        y_out.append((y * normw_ref[:, cols]).astype(y_ref.dtype))
        state_out.append(st * decay_chunk[:, cols] + _dot(b_t, x_in[:, cols]))

    y_ref[...] = jnp.concatenate(y_out, axis=1)
    for g in range(M2_GROUPS):
        state_scr[g] = state_out[g]


def _ssd_call_parts(z, xbc, small, conv_w, conv_b, gbias, alog_row, dskip_row, norm_w, n, nc):
    row, tail = _seq_specs(n, nc)
    expand = np.zeros((LANES, M2_WIDTH), np.float32)
    for h in range(M2_HEADS):
        expand[SLOT_DT + h, h * M2_HEAD_DIM:(h + 1) * M2_HEAD_DIM] = 1.0
    in_specs = [row(M2_WIDTH), row(M2_CONV_DIM), tail(M2_CONV_DIM), row(LANES),
                _const_spec((CONV_K, M2_CONV_DIM)), _const_spec((1, M2_CONV_DIM)), _const_spec((1, LANES)), _const_spec((1, LANES)),
                _const_spec((1, M2_WIDTH)), _const_spec((1, M2_WIDTH)), _const_spec((LANES, M2_WIDTH))]
    args = (z, xbc, xbc, small, conv_w, conv_b.reshape(1, -1), gbias, alog_row, dskip_row, norm_w.reshape(1, -1),
            jnp.asarray(expand, BF16))
    scratch = [pltpu.VMEM((TAIL + n, M2_CONV_DIM), F32), pltpu.VMEM((M2_GROUPS, M2_STATE, M2_GROUP_WIDTH), F32)]
    return in_specs, args, row(M2_WIDTH), jax.ShapeDtypeStruct((z.shape[0], M2_WIDTH), BF16), scratch


def _mixers(ml_inputs, ssd_inputs, bsz):
    t = ml_inputs[0].shape[0]
    seq = t // bsz
    n = min(SEQ_CHUNK, seq)
    nc = seq // n
    ml_specs, ml_args, ml_out_spec, ml_out_shape, ml_scratch = _mlstm_call_parts(*ml_inputs, n, nc)
    ssd_specs, ssd_args, ssd_out_spec, ssd_out_shape, ssd_scratch = _ssd_call_parts(*ssd_inputs, n, nc)
    n_ml, n_ssd = len(ml_args), len(ssd_args)

    def body(*refs):
        ins, outs, scr = refs[:n_ml + n_ssd], refs[n_ml + n_ssd:n_ml + n_ssd + 2], refs[n_ml + n_ssd + 2:]
        _mlstm_kernel(*ins[:n_ml], outs[0], *scr[:len(ml_scratch)])
        _ssd_kernel(*ins[n_ml:], outs[1], *scr[len(ml_scratch):])

    return pl.pallas_call(
        body,
        grid=(bsz, nc),
        in_specs=ml_specs + ssd_specs,
        out_specs=[ml_out_spec, ssd_out_spec],
        out_shape=[ml_out_shape, ssd_out_shape],
        scratch_shapes=ml_scratch + ssd_scratch,
        compiler_params=_cparams("arbitrary", "arbitrary"),
        name="mixers",
    )(*ml_args, *ssd_args)


def _merge_kernel(x_ref, yg_ref, yb_ref, yc_ref, g_ref, wglu_ref, wa_ref, wb_ref, wc_ref, wout_ref, nmp_ref, out_ref, ys_scr):
    d = D_MODEL
    nchunks = yg_ref.shape[1]
    for jb in range(S5_GROUPS // GROUPS_PER_VREG):
        for half in range(S5_CHUNK // GROUPS_PER_VREG):
            v = [yg_ref[jb * GROUPS_PER_VREG + g, :, half * LANES:(half + 1) * LANES].astype(F32) for g in range(GROUPS_PER_VREG)]
            v = _block_transpose(v)
            for t in range(GROUPS_PER_VREG):
                ys_scr[jb, pl.ds(half * GROUPS_PER_VREG + t, nchunks, stride=S5_CHUNK), :] = v[t]
    ys5 = jnp.concatenate([ys_scr[jb] for jb in range(S5_WIDTH // LANES)], axis=1)
    ya = ys5 * jax.nn.sigmoid(_dot(ys5.astype(BF16), wglu_ref[...]))
    gate = lambda i: jax.nn.sigmoid(g_ref[:, i * d:(i + 1) * d].astype(F32))
    merged = (gate(0) * _dot(ya.astype(BF16), wa_ref[...]) + gate(1) * _dot(yb_ref[...], wb_ref[...])
              + gate(2) * _dot(yc_ref[...], wc_ref[...]))
    mix = _dot(merged.astype(BF16), wout_ref[...])
    out_ref[...] = x_ref[...] + _rms(mix, nmp_ref[...])


def _merge(x2, yg, yb, yc, gates, weights, norm_post, layer):
    t = x2.shape[0]
    tm = min(TOKEN_BLOCK, t)
    row = lambda w: pl.BlockSpec((tm, w), lambda i: (i, 0))
    return pl.pallas_call(
        _merge_kernel,
        grid=(t // tm,),
        in_specs=[row(D_MODEL), pl.BlockSpec((S5_GROUPS, tm // S5_CHUNK, S5_K), lambda i: (0, i, 0)),
                  row(ML_WIDTH), row(M2_WIDTH), row(N_BRANCH * D_MODEL)]
        + [_layer_spec(w, layer) for w in weights] + [_const_spec((1, D_MODEL))],
        out_specs=row(D_MODEL),
        out_shape=jax.ShapeDtypeStruct((t, D_MODEL), F32),
        scratch_shapes=[pltpu.VMEM((S5_WIDTH // LANES, tm, LANES), F32)],
        compiler_params=_cparams("parallel"),
        name="merge",
    )(x2, yg, yb, yc, gates, *weights, norm_post.reshape(1, D_MODEL))


def _ffn_kernel(x_ref, npre_ref, npost_ref, wg_ref, wu_ref, wd_ref, out_ref):
    x = x_ref[...]
    hb = _rms(x, npre_ref[...]).astype(BF16)
    act = _silu(_dot(hb, wg_ref[...])) * _dot(hb, wu_ref[...])
    ffn = _dot(act.astype(BF16), wd_ref[...])
    out_ref[...] = x + _rms(ffn, npost_ref[...])


def _ffn(x2, weights, norm_pre, norm_post, layer):
    t = x2.shape[0]
    tm = min(TOKEN_BLOCK, t)
    row = pl.BlockSpec((tm, D_MODEL), lambda i: (i, 0))
    return pl.pallas_call(
        _ffn_kernel,
        grid=(t // tm,),
        in_specs=[row, _const_spec((1, D_MODEL)), _const_spec((1, D_MODEL))] + [_layer_spec(w, layer) for w in weights],
        out_specs=row,
        out_shape=jax.ShapeDtypeStruct((t, D_MODEL), F32),
        compiler_params=_cparams("parallel"),
        name="ffn",
    )(x2, norm_pre.reshape(1, D_MODEL), norm_post.reshape(1, D_MODEL), *weights)


def _gate_bias_row(bias_i, bias_f, dt_bias):
    pad = jnp.zeros((LANES - 2 * ML_HEADS - M2_HEADS,), F32)
    return jnp.concatenate([bias_i, bias_f, dt_bias, pad]).reshape(1, LANES)


def _head_lane_row(v):
    pad_l = jnp.zeros((SLOT_DT,), F32)
    pad_r = jnp.zeros((LANES - SLOT_DT - M2_HEADS,), F32)
    return jnp.concatenate([pad_l, v, pad_r]).reshape(1, LANES)


def _layer(x2, bsz, p, big, layer):
    ug, qk, v, o, z, xbc, gates, small = _in_proj(x2, p["norm_mix_pre"], big["w_in"], layer)
    mats = _s5_prep(p["s5_a_re"], p["s5_a_im"], p["s5_b_re"], p["s5_b_im"], p["s5_c_re"], p["s5_c_im"], p["s5_d"], p["s5_log_dt"])
    yg = _s5_mixer(ug, bsz, mats)
    gbias = _gate_bias_row(p["ml_bias_i"], p["ml_bias_f"], p["m2_dt_bias"])
    yb, yc = _mixers(
        (qk, v, o, small, p["ml_conv_w"], p["ml_conv_b"], gbias),
        (z, xbc, small, p["m2_conv_w"], p["m2_conv_b"], gbias, _head_lane_row(p["m2_a_log"]),
         jnp.repeat(p["m2_d"], M2_HEAD_DIM).reshape(1, M2_WIDTH), p["m2_norm_w"]),
        bsz)
    x1 = _merge(x2, yg, yb, yc, gates, big["merge"], p["norm_mix_post"], layer)
    return _ffn(x1, big["ffn"], p["norm_ffn_pre"], p["norm_ffn_post"], layer)


_PARAM_NAMES = ("norm_mix_pre", "norm_mix_post", "w_in", "s5_a_re", "s5_a_im", "s5_b_re", "s5_b_im", "s5_c_re", "s5_c_im", "s5_d",
                "s5_log_dt", "s5_w_glu", "ml_conv_w", "ml_conv_b", "ml_bias_i", "ml_bias_f", "m2_conv_w", "m2_conv_b", "m2_dt_bias",
                "m2_a_log", "m2_d", "m2_norm_w", "w_br_a", "w_br_b", "w_br_c", "w_out", "norm_ffn_pre", "norm_ffn_post",
                "w_ffn_gate", "w_ffn_up", "w_ffn_down")


def kernel(x, norm_mix_pre, norm_mix_post, w_in, s5_a_re, s5_a_im, s5_b_re, s5_b_im, s5_c_re, s5_c_im, s5_d, s5_log_dt, s5_w_glu, ml_conv_w, ml_conv_b, ml_bias_i, ml_bias_f, m2_conv_w, m2_conv_b, m2_dt_bias, m2_a_log, m2_d, m2_norm_w, w_br_a, w_br_b, w_br_c, w_out, norm_ffn_pre, norm_ffn_post, w_ffn_gate, w_ffn_up, w_ffn_down):
    stacked = (norm_mix_pre, norm_mix_post, w_in, s5_a_re, s5_a_im, s5_b_re, s5_b_im, s5_c_re, s5_c_im, s5_d, s5_log_dt, s5_w_glu,
               ml_conv_w, ml_conv_b, ml_bias_i, ml_bias_f, m2_conv_w, m2_conv_b, m2_dt_bias, m2_a_log, m2_d, m2_norm_w,
               w_br_a, w_br_b, w_br_c, w_out, norm_ffn_pre, norm_ffn_post, w_ffn_gate, w_ffn_up, w_ffn_down)
    bsz, seq, d = x.shape
    x2 = x.reshape(bsz * seq, d)
    big = {
        "w_in": _split_w_in(w_in),
        "merge": [w.astype(BF16) for w in (s5_w_glu, w_br_a, w_br_b, w_br_c, w_out)],
        "ffn": [w.astype(BF16) for w in (w_ffn_gate, w_ffn_up, w_ffn_down)],
    }
    for layer in range(norm_mix_pre.shape[0]):
        x2 = _layer(x2, bsz, {name: arr[layer] for name, arr in zip(_PARAM_NAMES, stacked)}, big, layer)
    return x2.reshape(bsz, seq, d)
```

```python
import jax
import jax.numpy as jnp
import numpy as np
from jax import lax
from jax.experimental import pallas as pl
from jax.experimental.pallas import tpu as pltpu

F32 = jnp.float32
BF16 = jnp.bfloat16

D_MODEL = 1024
EPS = 1e-6
CONV_K = 4

S5_WIDTH = 512
S5_GROUP = 16
S5_GROUPS = S5_WIDTH // S5_GROUP
S5_STATE = 64
S5_CHUNK = 16
S5_K = S5_CHUNK * S5_GROUP
S5_CHUNK_BLOCK = 256
S5_PREP_GROUPS = 4

ML_WIDTH = 512
ML_HEADS = 4
ML_HEAD_DIM = ML_WIDTH // ML_HEADS

M2_WIDTH = 1024
M2_HEAD_DIM = 64
M2_HEADS = M2_WIDTH // M2_HEAD_DIM
M2_GROUPS = 2
M2_HPG = M2_HEADS // M2_GROUPS
M2_STATE = 128
M2_CONV_DIM = M2_WIDTH + 2 * M2_GROUPS * M2_STATE
M2_GROUP_WIDTH = M2_WIDTH // M2_GROUPS

N_BRANCH = 3
FFN_HIDDEN = -(-(8 * D_MODEL) // (3 * 256)) * 256

LANES = 128
SUBLANES = 8
TAIL = SUBLANES
GROUPS_PER_VREG = LANES // S5_GROUP

SLOT_I = 0
SLOT_F = ML_HEADS
SLOT_DT = 2 * ML_HEADS

SEQ_CHUNK = 128
TOKEN_BLOCK = 512

VMEM_LIMIT = 56 * 1024 * 1024


def _cparams(*sem):
    return pltpu.CompilerParams(dimension_semantics=sem, vmem_limit_bytes=VMEM_LIMIT)


def _const_spec(shape):
    nd = len(shape)
    return pl.BlockSpec(shape, lambda *_: (0,) * nd, pipeline_mode=pl.Buffered(1))


def _layer_spec(stacked, layer):
    shape = stacked.shape[1:]
    zeros = (0,) * len(shape)
    return pl.BlockSpec((None,) + shape, lambda *_: (layer,) + zeros, pipeline_mode=pl.Buffered(1))


def _rms(x, w):
    return x * lax.rsqrt(jnp.mean(x * x, axis=-1, keepdims=True) + EPS) * w


def _split3(x):
    hi = x.astype(BF16)
    r1 = x - hi.astype(F32)
    mid = r1.astype(BF16)
    lo = (r1 - mid.astype(F32)).astype(BF16)
    return hi, mid, lo


def _dot(a, b):
    return jnp.dot(a, b, preferred_element_type=F32)


def _dot_nt(a, b):
    return lax.dot_general(a, b, (((1,), (1,)), ((), ())), preferred_element_type=F32)


def _dot3_left(m_bf16, x):
    hi, mid, lo = _split3(x)
    return _dot(m_bf16, hi) + _dot(m_bf16, mid) + _dot(m_bf16, lo)


def _dot3_right(x, m_bf16):
    hi, mid, lo = _split3(x)
    return _dot(hi, m_bf16) + _dot(mid, m_bf16) + _dot(lo, m_bf16)


def _lower_tri(n):
    r = lax.broadcasted_iota(jnp.int32, (n, n), 0)
    c = lax.broadcasted_iota(jnp.int32, (n, n), 1)
    return r >= c


def _block_transpose(v):
    lane_blk = lax.broadcasted_iota(jnp.int32, v[0].shape, 1) // S5_GROUP
    b = GROUPS_PER_VREG // 2
    while b:
        bit_set = (lane_blk & b) != 0
        out = list(v)
        for r in range(GROUPS_PER_VREG):
            if r & b == 0:
                rp = r | b
                out[r] = jnp.where(bit_set, pltpu.roll(v[rp], b * S5_GROUP, 1), v[r])
                out[rp] = jnp.where(bit_set, v[rp], pltpu.roll(v[r], LANES - b * S5_GROUP, 1))
        v = out
        b //= 2
    return v


IN_SEGMENTS = (
    ("qk", 2 * ML_WIDTH, BF16),
    ("v", ML_WIDTH, BF16),
    ("o", ML_WIDTH, BF16),
    ("z", M2_WIDTH, BF16),
    ("xbc", M2_CONV_DIM, BF16),
    ("gates", N_BRANCH * D_MODEL, BF16),
    ("small", LANES, F32),
)


W_A_WIDTH = S5_WIDTH + 4 * ML_WIDTH


def _split_w_in(w_in):
    splits = (S5_WIDTH, ML_WIDTH, ML_WIDTH, ML_WIDTH, ML_WIDTH, ML_HEADS, ML_HEADS, M2_WIDTH, M2_CONV_DIM, M2_HEADS, N_BRANCH * D_MODEL)
    offs = np.cumsum((0,) + splits)
    w16 = w_in.astype(BF16)
    col = lambda n0, n1: w16[..., offs[n0]:offs[n1]]
    pad = jnp.zeros(w_in.shape[:-1] + (LANES - 2 * ML_HEADS - M2_HEADS,), BF16)
    small = jnp.concatenate([col(5, 7), col(9, 10), pad], axis=-1)
    return [w16, col(7, 8), col(8, 9), col(10, 11), small]


def _silu(x):
    return x * jax.nn.sigmoid(x)


def _causal_conv(x_ref, tail_ref, w_ref, b_ref, pad_scr, first):
    n = x_ref.shape[0]
    tail = tail_ref[...].astype(F32)
    pad_scr[0:TAIL, :] = jnp.where(first, 0.0, tail)
    pad_scr[TAIL:TAIL + n, :] = x_ref[...].astype(F32)
    acc = b_ref[...] + w_ref[CONV_K - 1:CONV_K, :] * pad_scr[TAIL:TAIL + n, :]
    for j in range(CONV_K - 1):
        acc = acc + w_ref[j:j + 1, :] * pad_scr[pl.ds(TAIL - (CONV_K - 1) + j, n), :]
    return acc


def _in_proj_kernel(x_ref, nw_ref, wa_ref, wz_ref, wxbc_ref, wg_ref, wsmall_ref,
                    ug_ref, qk_ref, v_ref, o_ref, z_ref, xbc_ref, g_ref, small_ref, u_scr):
    hb = _rms(x_ref[...], nw_ref[...]).astype(BF16)
    u = _dot(hb, wa_ref[:, 0:S5_WIDTH])
    off = S5_WIDTH
    for ref in (qk_ref, v_ref, o_ref):
        width = ref.shape[1]
        ref[...] = _dot(hb, wa_ref[:, off:off + width]).astype(ref.dtype)
        off += width
    for ref, w_ref in ((z_ref, wz_ref), (xbc_ref, wxbc_ref), (g_ref, wg_ref), (small_ref, wsmall_ref)):
        ref[...] = _dot(hb, w_ref[...]).astype(ref.dtype)

    nchunks = u_scr.shape[1] // S5_CHUNK
    for jb in range(S5_GROUPS // GROUPS_PER_VREG):
        u_scr[jb] = u[:, jb * LANES:(jb + 1) * LANES]
        for half in range(S5_CHUNK // GROUPS_PER_VREG):
            v = [u_scr[jb, pl.ds(half * GROUPS_PER_VREG + t, nchunks, stride=S5_CHUNK), :] for t in range(GROUPS_PER_VREG)]
            v = _block_transpose(v)
            for g in range(GROUPS_PER_VREG):
                ug_ref[jb * GROUPS_PER_VREG + g, :, half * LANES:(half + 1) * LANES] = v[g].astype(ug_ref.dtype)


def _in_proj(x2, norm_w, w_parts, layer):
    t = x2.shape[0]
    tm = min(TOKEN_BLOCK, t)
    nchunks = tm // S5_CHUNK
    return pl.pallas_call(
        _in_proj_kernel,
        grid=(t // tm,),
        in_specs=[pl.BlockSpec((tm, D_MODEL), lambda i: (i, 0)), _const_spec((1, D_MODEL)),
                  pl.BlockSpec((None, D_MODEL, W_A_WIDTH), lambda i: (layer, 0, 0), pipeline_mode=pl.Buffered(1))]
        + [_layer_spec(w, layer) for w in w_parts[1:]],
        out_specs=[pl.BlockSpec((S5_GROUPS, nchunks, S5_K), lambda i: (0, i, 0))]
        + [pl.BlockSpec((tm, w), lambda i: (i, 0)) for _, w, _ in IN_SEGMENTS],
        out_shape=[jax.ShapeDtypeStruct((S5_GROUPS, t // S5_CHUNK, S5_K), BF16)]
        + [jax.ShapeDtypeStruct((t, w), dt) for _, w, dt in IN_SEGMENTS],
        scratch_shapes=[pltpu.VMEM((S5_WIDTH // LANES, tm, LANES), F32)],
        compiler_params=_cparams("parallel"),
        name="in_proj",
    )(x2, norm_w.reshape(1, D_MODEL), *w_parts)


def _s5_prep_kernel(*refs):
    for gi in range(refs[0].shape[0]):
        _s5_prep_group(*[r.at[gi:gi + 1] for r in refs])


def _s5_prep_group(lr_ref, li_ref, logdt_ref, bt_ref, bti_ref, btr_t_ref, bti_t_ref, cr_ref, ci_ref, d_ref,
                   tt_ref, w_ref, mt_ref, pq_ref):
    p64 = S5_STATE
    lr, li = lr_ref[0], li_ref[0]
    dt = jnp.exp(logdt_ref[0])
    mag = jnp.exp(lr * dt)
    lb_re = mag * jnp.cos(li * dt)
    lb_im = mag * jnp.sin(li * dt)
    den = lr * lr + li * li
    f_re = ((lb_re - 1.0) * lr + lb_im * li) / den
    f_im = (lb_im * lr - (lb_re - 1.0) * li) / den

    pw = [(jnp.ones_like(lb_re), jnp.zeros_like(lb_im))]
    for _ in range(S5_CHUNK):
        pr, pi = pw[-1]
        pw.append((pr * lb_re - pi * lb_im, pr * lb_im + pi * lb_re))

    cr, ci = cr_ref[0], ci_ref[0]
    btr_t, bti_t = btr_t_ref[0], bti_t_ref[0]
    g_re, g_im, m_re, m_im, w_re, w_im = [], [], [], [], [], []
    for k in range(S5_CHUNK):
        pr, pi = pw[k]
        afr = pr * f_re - pi * f_im
        afi = pr * f_im + pi * f_re
        g_re.append(afr * cr - afi * ci)
        g_im.append(afr * ci + afi * cr)
        qr, qi = pw[k + 1]
        m_re.append(qr * cr - qi * ci)
        m_im.append(qr * ci + qi * cr)
        sr, si = pw[S5_CHUNK - 1 - k]
        wfr = sr * f_re - si * f_im
        wfi = sr * f_im + si * f_re
        w_re.append(wfr * btr_t - wfi * bti_t)
        w_im.append(wfr * bti_t + wfi * btr_t)
    cat = lambda blocks: jnp.concatenate(blocks, axis=0)
    w_ref[0, :, 0:p64] = cat(w_re).astype(w_ref.dtype)
    w_ref[0, :, p64:2 * p64] = cat(w_im).astype(w_ref.dtype)
    mt_ref[0, :, 0:p64] = cat(m_re).astype(mt_ref.dtype)
    mt_ref[0, :, p64:2 * p64] = (-cat(m_im)).astype(mt_ref.dtype)

    hi = lax.Precision.HIGHEST
    kk = (jnp.dot(cat(g_re), bt_ref[0], precision=hi, preferred_element_type=F32)
          - jnp.dot(cat(g_im), bti_ref[0], precision=hi, preferred_element_type=F32))
    rows = lax.broadcasted_iota(jnp.int32, (S5_K, S5_K), 0)
    cols = lax.broadcasted_iota(jnp.int32, (S5_K, S5_K), 1)
    col_blk = cols // S5_GROUP
    tt = jnp.where(rows == cols, d_ref[0], 0.0)
    for s in range(S5_CHUNK):
        if s == 0:
            shifted = kk
        else:
            shifted = jnp.concatenate([jnp.zeros((s * S5_GROUP, S5_K), F32), kk[:S5_K - s * S5_GROUP, :]], axis=0)
        tt = tt + jnp.where(col_blk == s, shifted, 0.0)
    tt_ref[0] = tt.astype(tt_ref.dtype)

    cr16, ci16 = pw[S5_CHUNK]
    pq_ref[0, 0:1, :] = jnp.concatenate([cr16, cr16], axis=1)
    pq_ref[0, 1:2, :] = jnp.concatenate([-ci16, ci16], axis=1)


def _s5_prep(a_re, a_im, b_re, b_im, c_re, c_im, d_skip, log_dt):
    g, p = S5_GROUPS, S5_STATE
    tile_lanes = lambda b: jnp.tile(b, (1, 1, S5_CHUNK))
    swap = lambda b: jnp.transpose(b, (0, 2, 1))
    args = (
        a_re.reshape(g, 1, p), a_im.reshape(g, 1, p), log_dt.reshape(g, 1, 1),
        tile_lanes(b_re), tile_lanes(b_im), swap(b_re), swap(b_im), c_re, c_im,
        jnp.tile(d_skip, (1, S5_CHUNK)).reshape(g, 1, S5_K),
    )
    spec = lambda a: pl.BlockSpec((S5_PREP_GROUPS,) + a.shape[1:], lambda i: (i, 0, 0))
    out_shapes = (
        jax.ShapeDtypeStruct((g, S5_K, S5_K), BF16),
        jax.ShapeDtypeStruct((g, S5_K, 2 * p), BF16),
        jax.ShapeDtypeStruct((g, S5_K, 2 * p), BF16),
        jax.ShapeDtypeStruct((g, 2, 2 * p), F32),
    )
    return pl.pallas_call(
        _s5_prep_kernel,
        grid=(g // S5_PREP_GROUPS,),
        in_specs=[spec(a) for a in args],
        out_specs=[spec(o) for o in out_shapes],
        out_shape=out_shapes,
        compiler_params=_cparams("parallel"),
        name="s5_prep",
    )(*args)


def _s5_kernel(ug_ref, tt_ref, w_ref, mt_ref, pq_ref, yg_ref, l_scr, s_scr, state_scr):
    ng = S5_GROUPS
    nchunks = ug_ref.shape[1]

    @pl.when(pl.program_id(1) == 0)
    def _():
        state_scr[...] = jnp.zeros_like(state_scr)

    def local(g, carry):
        l_scr[pl.ds(g, nchunks, stride=ng), :] = _dot(ug_ref[g], w_ref[g])
        return carry

    lax.fori_loop(0, ng, local, 0, unroll=8)

    p_mul = pq_ref[0]
    q_mul = pq_ref[1]

    def step(j, carry):
        x, xs = carry
        row = pl.multiple_of(j * ng, ng)
        s_scr[pl.ds(row, ng), :] = x
        l = l_scr[pl.ds(row, ng), :]
        ls = pltpu.roll(l, S5_STATE, 1)
        return p_mul * x + q_mul * xs + l, p_mul * xs - q_mul * x + ls

    x, xs = lax.fori_loop(0, nchunks, step, (state_scr[0], state_scr[1]), unroll=8)
    state_scr[0] = x
    state_scr[1] = xs

    def output(g, carry):
        u = ug_ref[g]
        s_in = s_scr[pl.ds(g, nchunks, stride=ng), :].astype(BF16)
        y = _dot_nt(u, tt_ref[g]) + _dot_nt(s_in, mt_ref[g])
        yg_ref[g] = jax.nn.gelu(y).astype(yg_ref.dtype)
        return carry

    lax.fori_loop(0, ng, output, 0, unroll=8)


def _s5_mixer(ug, bsz, mats):
    tt, w_mat, mt, pq = mats
    pq = jnp.transpose(pq, (1, 0, 2))
    g, nch, _ = ug.shape
    nch_b = nch // bsz
    cb = min(S5_CHUNK_BLOCK, nch_b)
    nblk = nch_b // cb
    blk = pl.BlockSpec((g, cb, S5_K), lambda b, j: (0, b * nblk + j, 0))
    return pl.pallas_call(
        _s5_kernel,
        grid=(bsz, nblk),
        in_specs=[blk, _const_spec(tt.shape), _const_spec(w_mat.shape), _const_spec(mt.shape), _const_spec(pq.shape)],
        out_specs=blk,
        out_shape=jax.ShapeDtypeStruct(ug.shape, BF16),
        scratch_shapes=[
            pltpu.VMEM((cb * g, 2 * S5_STATE), F32),
            pltpu.VMEM((cb * g, 2 * S5_STATE), F32),
            pltpu.VMEM((2, g, 2 * S5_STATE), F32),
        ],
        compiler_params=_cparams("arbitrary", "arbitrary"),
        name="s5",
    )(ug, tt, w_mat, mt, pq)


def _mlstm_kernel(qk_ref, tail_ref, v_ref, o_ref, small_ref, cw_ref, cb_ref, gbias_ref, y_ref, pad_scr, ct_scr, m_scr):
    first = pl.program_id(1) == 0
    n = qk_ref.shape[0]
    dh = ML_HEAD_DIM

    @pl.when(first)
    def _():
        ct_scr[...] = jnp.zeros_like(ct_scr)
        m_scr[...] = jnp.zeros_like(m_scr)

    ct_in = [ct_scr[h] for h in range(ML_HEADS)]
    m_in = [m_scr[h][:, 0:1] for h in range(ML_HEADS)]
    qk = _silu(_causal_conv(qk_ref, tail_ref, cw_ref, cb_ref, pad_scr, first))
    gate = small_ref[...] + gbias_ref[...]
    lf = jax.nn.log_sigmoid(gate)
    tri = _lower_tri(n)
    b_all = _dot3_left(tri.astype(BF16), lf)
    gate_t = gate.T
    b_t = b_all.T
    ones_col = (lax.broadcasted_iota(jnp.int32, (n, dh), 1) == 0).astype(BF16)

    outs, ct_out, m_out = [], [], []
    for h in range(ML_HEADS):
        q = qk[:, h * dh:(h + 1) * dh].astype(BF16)
        k = qk[:, ML_WIDTH + h * dh:ML_WIDTH + (h + 1) * dh] * (dh ** -0.5)
        k_t = k.T.astype(BF16)
        v_ext = jnp.concatenate([v_ref[:, h * dh:(h + 1) * dh], ones_col], axis=1)
        b_col = b_all[:, SLOT_F + h:SLOT_F + h + 1]
        ig_col = gate[:, SLOT_I + h:SLOT_I + h + 1]
        b_row = b_t[SLOT_F + h:SLOT_F + h + 1, :]
        ig_row = gate_t[SLOT_I + h:SLOT_I + h + 1, :]
        m_prev = m_in[h]
        ct_prev = ct_in[h]

        dmat = jnp.where(tri, b_col - b_row + ig_row, -jnp.inf)
        inter = b_col + m_prev
        m_t = jnp.maximum(jnp.max(dmat, axis=1, keepdims=True), inter)
        p = _dot(q, k_t) * jnp.exp(dmat - m_t)
        w_inter = jnp.exp(inter - m_t)
        num_den = _dot(p.astype(BF16), v_ext) + w_inter * _dot(q, ct_prev.astype(BF16))
        den = num_den[:, dh:dh + 1]
        denom = jnp.maximum(jnp.abs(den), jnp.exp(-m_t))
        out = num_den[:, :dh] / denom * jax.nn.sigmoid(o_ref[:, h * dh:(h + 1) * dh].astype(F32))
        outs.append(out.astype(y_ref.dtype))

        b_end = b_col[n - 1:n, :]
        a_end = b_end - b_col + ig_col
        m_loc = jnp.max(a_end, axis=0, keepdims=True)
        w_end = jnp.exp(a_end - m_loc)
        m_new = jnp.maximum(b_end + m_prev, m_loc)
        s_prev = jnp.exp(b_end + m_prev - m_new)
        s_loc = jnp.exp(m_loc - m_new)
        upd = _dot(k_t, (w_end * v_ext.astype(F32)).astype(BF16))
        ct_out.append(s_prev * ct_prev + s_loc * upd)
        m_out.append(jnp.broadcast_to(m_new, m_scr.shape[1:]))

    y_ref[...] = jnp.concatenate(outs, axis=1)
    for h in range(ML_HEADS):
        ct_scr[h] = ct_out[h]
        m_scr[h] = m_out[h]


def _seq_specs(n, nc):
    row = lambda w: pl.BlockSpec((n, w), lambda b, c: (b * nc + c, 0))
    tail = lambda w: pl.BlockSpec((TAIL, w), lambda b, c: (jnp.maximum((b * nc + c) * (n // TAIL) - 1, 0), 0))
    return row, tail


def _mlstm_call_parts(qk, v, o, small, conv_w, conv_b, gbias, n, nc):
    row, tail = _seq_specs(n, nc)
    in_specs = [row(2 * ML_WIDTH), tail(2 * ML_WIDTH), row(ML_WIDTH), row(ML_WIDTH), row(LANES),
                _const_spec((CONV_K, 2 * ML_WIDTH)), _const_spec((1, 2 * ML_WIDTH)), _const_spec((1, LANES))]
    args = (qk, qk, v, o, small, conv_w, conv_b.reshape(1, -1), gbias)
    scratch = [pltpu.VMEM((TAIL + n, 2 * ML_WIDTH), F32),
               pltpu.VMEM((ML_HEADS, ML_HEAD_DIM, 2 * ML_HEAD_DIM), F32),
               pltpu.VMEM((ML_HEADS, 1, LANES), F32)]
    return in_specs, args, row(ML_WIDTH), jax.ShapeDtypeStruct((qk.shape[0], ML_WIDTH), BF16), scratch


def _ssd_kernel(z_ref, xbc_ref, tail_ref, small_ref, cw_ref, cb_ref, gbias_ref, alog_ref, dskip_ref, normw_ref, expand_ref,
                y_ref, pad_scr, state_scr):
    first = pl.program_id(1) == 0
    n = z_ref.shape[0]
    gw = M2_GROUP_WIDTH
    hd = M2_HEAD_DIM

    @pl.when(first)
    def _():
        state_scr[...] = jnp.zeros_like(state_scr)

    state_in = [state_scr[g] for g in range(M2_GROUPS)]
    xbc = _silu(_causal_conv(xbc_ref, tail_ref, cw_ref, cb_ref, pad_scr, first))
    xs = xbc[:, :M2_WIDTH]
    lane = lax.broadcasted_iota(jnp.int32, (1, LANES), 1)
    dt_lanes = (lane >= SLOT_DT) & (lane < SLOT_DT + M2_HEADS)
    dt = jnp.where(dt_lanes, jax.nn.softplus(small_ref[...] + gbias_ref[...]), 0.0)
    a_row = jnp.where(dt_lanes, -jnp.exp(alog_ref[...]), 0.0)
    tri = _lower_tri(n)
    cum = _dot3_left(tri.astype(BF16), dt * a_row)
    cum_t = cum.T
    cum_end = cum[n - 1:n, :]
    expand = expand_ref[...]
    dt_e = _dot(dt.astype(BF16), expand)
    decay_in = _dot(jnp.exp(cum).astype(BF16), expand)
    decay_out = _dot((dt * jnp.exp(cum_end - cum)).astype(BF16), expand)
    decay_chunk = _dot3_right(jnp.exp(cum[n - SUBLANES:n, :]), expand)[SUBLANES - 1:SUBLANES, :]
    xdt_b = (xs * dt_e).astype(BF16)
    x_in = (xs * decay_out).astype(BF16)
    left_half = lax.broadcasted_iota(jnp.int32, (n, 2 * hd), 1) < hd

    y_out, state_out = [], []
    for g in range(M2_GROUPS):
        b_g = xbc[:, M2_WIDTH + g * M2_STATE:M2_WIDTH + (g + 1) * M2_STATE]
        c_g = xbc[:, M2_WIDTH + (M2_GROUPS + g) * M2_STATE:M2_WIDTH + (M2_GROUPS + g + 1) * M2_STATE].astype(BF16)
        b_t = b_g.T.astype(BF16)
        cb = _dot(c_g, b_t)
        y_diag = []
        for pair in range(M2_HPG // 2):
            head = g * M2_HPG + 2 * pair
            ws = []
            for hh in (head, head + 1):
                cum_col = cum[:, SLOT_DT + hh:SLOT_DT + hh + 1]
                cum_row = cum_t[SLOT_DT + hh:SLOT_DT + hh + 1, :]
                ws.append((cb * jnp.exp(jnp.where(tri, cum_col - cum_row, -jnp.inf))).astype(BF16))
            xp = xdt_b[:, head * hd:(head + 2) * hd]
            zero = jnp.zeros_like(xp)
            rhs = jnp.concatenate([jnp.where(left_half, xp, zero), jnp.where(left_half, zero, xp)], axis=0)
            y_diag.append(_dot(jnp.concatenate(ws, axis=1), rhs))
        cols = slice(g * gw, (g + 1) * gw)
        st = state_in[g]
        y_off = _dot(c_g, st.astype(BF16)) * decay_in[:, cols]
        y = jnp.concatenate(y_diag, axis=1) + y_off + dskip_ref[:, cols] * xs[:, cols]
        y = y * _silu(z_ref[:, cols].astype(F32))
        y = y * lax.rsqrt(jnp.mean(y * y, axis=-1, keepdims=True) + EPS)
        y_out.append((y * normw_ref[:, cols]).astype(y_ref.dtype))
        state_out.append(st * decay_chunk[:, cols] + _dot(b_t, x_in[:, cols]))

    y_ref[...] = jnp.concatenate(y_out, axis=1)
    for g in range(M2_GROUPS):
        state_scr[g] = state_out[g]


def _ssd_call_parts(z, xbc, small, conv_w, conv_b, gbias, alog_row, dskip_row, norm_w, n, nc):
    row, tail = _seq_specs(n, nc)
    expand = np.zeros((LANES, M2_WIDTH), np.float32)
    for h in range(M2_HEADS):
        expand[SLOT_DT + h, h * M2_HEAD_DIM:(h + 1) * M2_HEAD_DIM] = 1.0
    in_specs = [row(M2_WIDTH), row(M2_CONV_DIM), tail(M2_CONV_DIM), row(LANES),
                _const_spec((CONV_K, M2_CONV_DIM)), _const_spec((1, M2_CONV_DIM)), _const_spec((1, LANES)), _const_spec((1, LANES)),
                _const_spec((1, M2_WIDTH)), _const_spec((1, M2_WIDTH)), _const_spec((LANES, M2_WIDTH))]
    args = (z, xbc, xbc, small, conv_w, conv_b.reshape(1, -1), gbias, alog_row, dskip_row, norm_w.reshape(1, -1),
            jnp.asarray(expand, BF16))
    scratch = [pltpu.VMEM((TAIL + n, M2_CONV_DIM), F32), pltpu.VMEM((M2_GROUPS, M2_STATE, M2_GROUP_WIDTH), F32)]
    return in_specs, args, row(M2_WIDTH), jax.ShapeDtypeStruct((z.shape[0], M2_WIDTH), BF16), scratch


def _mixers(ml_inputs, ssd_inputs, bsz):
    t = ml_inputs[0].shape[0]
    seq = t // bsz
    n = min(SEQ_CHUNK, seq)
    nc = seq // n
    ml_specs, ml_args, ml_out_spec, ml_out_shape, ml_scratch = _mlstm_call_parts(*ml_inputs, n, nc)
    ssd_specs, ssd_args, ssd_out_spec, ssd_out_shape, ssd_scratch = _ssd_call_parts(*ssd_inputs, n, nc)
    n_ml, n_ssd = len(ml_args), len(ssd_args)

    def body(*refs):
        ins, outs, scr = refs[:n_ml + n_ssd], refs[n_ml + n_ssd:n_ml + n_ssd + 2], refs[n_ml + n_ssd + 2:]
        _mlstm_kernel(*ins[:n_ml], outs[0], *scr[:len(ml_scratch)])
        _ssd_kernel(*ins[n_ml:], outs[1], *scr[len(ml_scratch):])

    return pl.pallas_call(
        body,
        grid=(bsz, nc),
        in_specs=ml_specs + ssd_specs,
        out_specs=[ml_out_spec, ssd_out_spec],
        out_shape=[ml_out_shape, ssd_out_shape],
        scratch_shapes=ml_scratch + ssd_scratch,
        compiler_params=_cparams("arbitrary", "arbitrary"),
        name="mixers",
    )(*ml_args, *ssd_args)


def _merge_kernel(x_ref, yg_ref, yb_ref, yc_ref, g_ref, wglu_ref, wa_ref, wb_ref, wc_ref, wout_ref, nmp_ref, out_ref, ys_scr):
    d = D_MODEL
    gate = lambda i: jax.nn.sigmoid(g_ref[:, i * d:(i + 1) * d].astype(F32))
    merged_bc = gate(1) * _dot(yb_ref[...], wb_ref[...]) + gate(2) * _dot(yc_ref[...], wc_ref[...])
    nchunks = yg_ref.shape[1]
    for jb in range(S5_GROUPS // GROUPS_PER_VREG):
        for half in range(S5_CHUNK // GROUPS_PER_VREG):
            v = [yg_ref[jb * GROUPS_PER_VREG + g, :, half * LANES:(half + 1) * LANES].astype(F32) for g in range(GROUPS_PER_VREG)]
            v = _block_transpose(v)
            for t in range(GROUPS_PER_VREG):
                ys_scr[jb, pl.ds(half * GROUPS_PER_VREG + t, nchunks, stride=S5_CHUNK), :] = v[t]
    ys5 = jnp.concatenate([ys_scr[jb] for jb in range(S5_WIDTH // LANES)], axis=1)
    ya = ys5 * jax.nn.sigmoid(_dot(ys5.astype(BF16), wglu_ref[...]))
    merged = gate(0) * _dot(ya.astype(BF16), wa_ref[...]) + merged_bc
    mix = _dot(merged.astype(BF16), wout_ref[...])
    out_ref[...] = x_ref[...] + _rms(mix, nmp_ref[...])


def _merge(x2, yg, yb, yc, gates, weights, norm_post, layer):
    t = x2.shape[0]
    tm = min(TOKEN_BLOCK, t)
    row = lambda w: pl.BlockSpec((tm, w), lambda i: (i, 0))
    return pl.pallas_call(
        _merge_kernel,
        grid=(t // tm,),
        in_specs=[row(D_MODEL), pl.BlockSpec((S5_GROUPS, tm // S5_CHUNK, S5_K), lambda i: (0, i, 0)),
                  row(ML_WIDTH), row(M2_WIDTH), row(N_BRANCH * D_MODEL)]
        + [_layer_spec(w, layer) for w in weights] + [_const_spec((1, D_MODEL))],
        out_specs=row(D_MODEL),
        out_shape=jax.ShapeDtypeStruct((t, D_MODEL), F32),
        scratch_shapes=[pltpu.VMEM((S5_WIDTH // LANES, tm, LANES), F32)],
        compiler_params=_cparams("parallel"),
        name="merge",
    )(x2, yg, yb, yc, gates, *weights, norm_post.reshape(1, D_MODEL))


def _ffn_kernel(x_ref, npre_ref, npost_ref, wg_ref, wu_ref, wd_ref, out_ref):
    x = x_ref[...]
    hb = _rms(x, npre_ref[...]).astype(BF16)
    act = _silu(_dot(hb, wg_ref[...])) * _dot(hb, wu_ref[...])
    ffn = _dot(act.astype(BF16), wd_ref[...])
    out_ref[...] = x + _rms(ffn, npost_ref[...])


def _ffn(x2, weights, norm_pre, norm_post, layer):
    t = x2.shape[0]
    tm = min(TOKEN_BLOCK, t)
    row = pl.BlockSpec((tm, D_MODEL), lambda i: (i, 0))
    return pl.pallas_call(
        _ffn_kernel,
        grid=(t // tm,),
        in_specs=[row, _const_spec((1, D_MODEL)), _const_spec((1, D_MODEL))] + [_layer_spec(w, layer) for w in weights],
        out_specs=row,
        out_shape=jax.ShapeDtypeStruct((t, D_MODEL), F32),
        compiler_params=_cparams("parallel"),
        name="ffn",
    )(x2, norm_pre.reshape(1, D_MODEL), norm_post.reshape(1, D_MODEL), *weights)


def _gate_bias_row(bias_i, bias_f, dt_bias):
    pad = jnp.zeros((LANES - 2 * ML_HEADS - M2_HEADS,), F32)
    return jnp.concatenate([bias_i, bias_f, dt_bias, pad]).reshape(1, LANES)


def _head_lane_row(v):
    pad_l = jnp.zeros((SLOT_DT,), F32)
    pad_r = jnp.zeros((LANES - SLOT_DT - M2_HEADS,), F32)
    return jnp.concatenate([pad_l, v, pad_r]).reshape(1, LANES)


def _layer(x2, bsz, p, big, layer):
    ug, qk, v, o, z, xbc, gates, small = _in_proj(x2, p["norm_mix_pre"], big["w_in"], layer)
    mats = _s5_prep(p["s5_a_re"], p["s5_a_im"], p["s5_b_re"], p["s5_b_im"], p["s5_c_re"], p["s5_c_im"], p["s5_d"], p["s5_log_dt"])
    yg = _s5_mixer(ug, bsz, mats)
    gbias = _gate_bias_row(p["ml_bias_i"], p["ml_bias_f"], p["m2_dt_bias"])
    yb, yc = _mixers(
        (qk, v, o, small, p["ml_conv_w"], p["ml_conv_b"], gbias),
        (z, xbc, small, p["m2_conv_w"], p["m2_conv_b"], gbias, _head_lane_row(p["m2_a_log"]),
         jnp.repeat(p["m2_d"], M2_HEAD_DIM).reshape(1, M2_WIDTH), p["m2_norm_w"]),
        bsz)
    x1 = _merge(x2, yg, yb, yc, gates, big["merge"], p["norm_mix_post"], layer)
    return _ffn(x1, big["ffn"], p["norm_ffn_pre"], p["norm_ffn_post"], layer)


_PARAM_NAMES = ("norm_mix_pre", "norm_mix_post", "w_in", "s5_a_re", "s5_a_im", "s5_b_re", "s5_b_im", "s5_c_re", "s5_c_im", "s5_d",
                "s5_log_dt", "s5_w_glu", "ml_conv_w", "ml_conv_b", "ml_bias_i", "ml_bias_f", "m2_conv_w", "m2_conv_b", "m2_dt_bias",
                "m2_a_log", "m2_d", "m2_norm_w", "w_br_a", "w_br_b", "w_br_c", "w_out", "norm_ffn_pre", "norm_ffn_post",
                "w_ffn_gate", "w_ffn_up", "w_ffn_down")


def kernel(x, norm_mix_pre, norm_mix_post, w_in, s5_a_re, s5_a_im, s5_b_re, s5_b_im, s5_c_re, s5_c_im, s5_d, s5_log_dt, s5_w_glu, ml_conv_w, ml_conv_b, ml_bias_i, ml_bias_f, m2_conv_w, m2_conv_b, m2_dt_bias, m2_a_log, m2_d, m2_norm_w, w_br_a, w_br_b, w_br_c, w_out, norm_ffn_pre, norm_ffn_post, w_ffn_gate, w_ffn_up, w_ffn_down):
    stacked = (norm_mix_pre, norm_mix_post, w_in, s5_a_re, s5_a_im, s5_b_re, s5_b_im, s5_c_re, s5_c_im, s5_d, s5_log_dt, s5_w_glu,
               ml_conv_w, ml_conv_b, ml_bias_i, ml_bias_f, m2_conv_w, m2_conv_b, m2_dt_bias, m2_a_log, m2_d, m2_norm_w,
               w_br_a, w_br_b, w_br_c, w_out, norm_ffn_pre, norm_ffn_post, w_ffn_gate, w_ffn_up, w_ffn_down)
    bsz, seq, d = x.shape
    x2 = x.reshape(bsz * seq, d)
    big = {
        "w_in": _split_w_in(w_in),
        "merge": [w.astype(BF16) for w in (s5_w_glu, w_br_a, w_br_b, w_br_c, w_out)],
        "ffn": [w.astype(BF16) for w in (w_ffn_gate, w_ffn_up, w_ffn_down)],
    }
    for layer in range(norm_mix_pre.shape[0]):
        x2 = _layer(x2, bsz, {name: arr[layer] for name, arr in zip(_PARAM_NAMES, stacked)}, big, layer)
    return x2.reshape(bsz, seq, d)
```

```python
import jax
import jax.numpy as jnp
import numpy as np
from jax import lax
from jax.experimental import pallas as pl
from jax.experimental.pallas import tpu as pltpu

F32 = jnp.float32
BF16 = jnp.bfloat16

D_MODEL = 1024
EPS = 1e-6
CONV_K = 4

S5_WIDTH = 512
S5_GROUP = 16
S5_GROUPS = S5_WIDTH // S5_GROUP
S5_STATE = 64
S5_CHUNK = 16
S5_K = S5_CHUNK * S5_GROUP
S5_CHUNK_BLOCK = 256
S5_PREP_GROUPS = 4

ML_WIDTH = 512
ML_HEADS = 4
ML_HEAD_DIM = ML_WIDTH // ML_HEADS

M2_WIDTH = 1024
M2_HEAD_DIM = 64
M2_HEADS = M2_WIDTH // M2_HEAD_DIM
M2_GROUPS = 2
M2_HPG = M2_HEADS // M2_GROUPS
M2_STATE = 128
M2_CONV_DIM = M2_WIDTH + 2 * M2_GROUPS * M2_STATE
M2_GROUP_WIDTH = M2_WIDTH // M2_GROUPS

N_BRANCH = 3
FFN_HIDDEN = -(-(8 * D_MODEL) // (3 * 256)) * 256

LANES = 128
SUBLANES = 8
TAIL = SUBLANES
GROUPS_PER_VREG = LANES // S5_GROUP

SLOT_I = 0
SLOT_F = ML_HEADS
SLOT_DT = 2 * ML_HEADS

SEQ_CHUNK = 128
TOKEN_BLOCK = 512

VMEM_LIMIT = 56 * 1024 * 1024


def _cparams(*sem):
    return pltpu.CompilerParams(dimension_semantics=sem, vmem_limit_bytes=VMEM_LIMIT)


def _const_spec(shape):
    nd = len(shape)
    return pl.BlockSpec(shape, lambda *_: (0,) * nd, pipeline_mode=pl.Buffered(1))


def _layer_spec(stacked, layer):
    shape = stacked.shape[1:]
    zeros = (0,) * len(shape)
    return pl.BlockSpec((None,) + shape, lambda *_: (layer,) + zeros, pipeline_mode=pl.Buffered(1))


def _rms(x, w):
    return x * lax.rsqrt(jnp.mean(x * x, axis=-1, keepdims=True) + EPS) * w


def _split3(x):
    hi = x.astype(BF16)
    r1 = x - hi.astype(F32)
    mid = r1.astype(BF16)
    lo = (r1 - mid.astype(F32)).astype(BF16)
    return hi, mid, lo


def _dot(a, b):
    return jnp.dot(a, b, preferred_element_type=F32)


def _dot_nt(a, b):
    return lax.dot_general(a, b, (((1,), (1,)), ((), ())), preferred_element_type=F32)


def _dot3_left(m_bf16, x):
    hi, mid, lo = _split3(x)
    return _dot(m_bf16, hi) + _dot(m_bf16, mid) + _dot(m_bf16, lo)


def _dot3_right(x, m_bf16):
    hi, mid, lo = _split3(x)
    return _dot(hi, m_bf16) + _dot(mid, m_bf16) + _dot(lo, m_bf16)


def _lower_tri(n):
    r = lax.broadcasted_iota(jnp.int32, (n, n), 0)
    c = lax.broadcasted_iota(jnp.int32, (n, n), 1)
    return r >= c


def _block_transpose(v):
    lane_blk = lax.broadcasted_iota(jnp.int32, v[0].shape, 1) // S5_GROUP
    b = GROUPS_PER_VREG // 2
    while b:
        bit_set = (lane_blk & b) != 0
        out = list(v)
        for r in range(GROUPS_PER_VREG):
            if r & b == 0:
                rp = r | b
                out[r] = jnp.where(bit_set, pltpu.roll(v[rp], b * S5_GROUP, 1), v[r])
                out[rp] = jnp.where(bit_set, v[rp], pltpu.roll(v[r], LANES - b * S5_GROUP, 1))
        v = out
        b //= 2
    return v


IN_SEGMENTS = (
    ("qk", 2 * ML_WIDTH, BF16),
    ("v", ML_WIDTH, BF16),
    ("o", ML_WIDTH, BF16),
    ("z", M2_WIDTH, BF16),
    ("xbc", M2_CONV_DIM, BF16),
    ("gates", N_BRANCH * D_MODEL, BF16),
    ("small", LANES, F32),
)


W_A_WIDTH = S5_WIDTH + 4 * ML_WIDTH


def _split_w_in(w_in):
    splits = (S5_WIDTH, ML_WIDTH, ML_WIDTH, ML_WIDTH, ML_WIDTH, ML_HEADS, ML_HEADS, M2_WIDTH, M2_CONV_DIM, M2_HEADS, N_BRANCH * D_MODEL)
    offs = np.cumsum((0,) + splits)
    w16 = w_in.astype(BF16)
    col = lambda n0, n1: w16[..., offs[n0]:offs[n1]]
    pad = jnp.zeros(w_in.shape[:-1] + (LANES - 2 * ML_HEADS - M2_HEADS,), BF16)
    small = jnp.concatenate([col(5, 7), col(9, 10), pad], axis=-1)
    return [w16, col(7, 8), col(8, 9), col(10, 11), small]


def _sigmoid(x):
    return 0.5 * jnp.tanh(0.5 * x) + 0.5


def _silu(x):
    h = 0.5 * x
    return h + h * jnp.tanh(h)


def _causal_conv(x_ref, tail_ref, w_ref, b_ref, first):
    n = x_ref.shape[0]
    x = x_ref[...].astype(F32)
    row = lax.broadcasted_iota(jnp.int32, (TAIL, 1), 0)
    shifts = [CONV_K - 1 - j for j in range(CONV_K - 1)]
    prev = jnp.where(first, 0.0, tail_ref[...].astype(F32))
    prev_rot = [pltpu.roll(prev, s, 0) for s in shifts]
    tiles = []
    for i in range(n // TAIL):
        cur = x[i * TAIL:(i + 1) * TAIL, :]
        cur_rot = [pltpu.roll(cur, s, 0) for s in shifts]
        acc = b_ref[...] + w_ref[CONV_K - 1:CONV_K, :] * cur
        for j, s in enumerate(shifts):
            acc = acc + w_ref[j:j + 1, :] * jnp.where(row < s, prev_rot[j], cur_rot[j])
        tiles.append(acc)
        prev_rot = cur_rot
    return jnp.concatenate(tiles, axis=0)


def _in_proj_kernel(x_ref, nw_ref, wa_ref, wz_ref, wxbc_ref, wg_ref, wsmall_ref,
                    ug_ref, qk_ref, v_ref, o_ref, z_ref, xbc_ref, g_ref, small_ref, u_scr):
    hb = _rms(x_ref[...], nw_ref[...]).astype(BF16)
    u = _dot(hb, wa_ref[:, 0:S5_WIDTH])
    off = S5_WIDTH
    for ref in (qk_ref, v_ref, o_ref):
        width = ref.shape[1]
        ref[...] = _dot(hb, wa_ref[:, off:off + width]).astype(ref.dtype)
        off += width
    for ref, w_ref in ((z_ref, wz_ref), (xbc_ref, wxbc_ref), (g_ref, wg_ref), (small_ref, wsmall_ref)):
        ref[...] = _dot(hb, w_ref[...]).astype(ref.dtype)

    nchunks = u_scr.shape[1] // S5_CHUNK
    for jb in range(S5_GROUPS // GROUPS_PER_VREG):
        u_scr[jb] = u[:, jb * LANES:(jb + 1) * LANES]
        for half in range(S5_CHUNK // GROUPS_PER_VREG):
            v = [u_scr[jb, pl.ds(half * GROUPS_PER_VREG + t, nchunks, stride=S5_CHUNK), :] for t in range(GROUPS_PER_VREG)]
            v = _block_transpose(v)
            for g in range(GROUPS_PER_VREG):
                ug_ref[jb * GROUPS_PER_VREG + g, :, half * LANES:(half + 1) * LANES] = v[g].astype(ug_ref.dtype)


def _in_proj(x2, norm_w, w_parts, layer):
    t = x2.shape[0]
    tm = min(TOKEN_BLOCK, t)
    nchunks = tm // S5_CHUNK
    return pl.pallas_call(
        _in_proj_kernel,
        grid=(t // tm,),
        in_specs=[pl.BlockSpec((tm, D_MODEL), lambda i: (i, 0)), _const_spec((1, D_MODEL)),
                  pl.BlockSpec((None, D_MODEL, W_A_WIDTH), lambda i: (layer, 0, 0), pipeline_mode=pl.Buffered(1))]
        + [_layer_spec(w, layer) for w in w_parts[1:]],
        out_specs=[pl.BlockSpec((S5_GROUPS, nchunks, S5_K), lambda i: (0, i, 0))]
        + [pl.BlockSpec((tm, w), lambda i: (i, 0)) for _, w, _ in IN_SEGMENTS],
        out_shape=[jax.ShapeDtypeStruct((S5_GROUPS, t // S5_CHUNK, S5_K), BF16)]
        + [jax.ShapeDtypeStruct((t, w), dt) for _, w, dt in IN_SEGMENTS],
        scratch_shapes=[pltpu.VMEM((S5_WIDTH // LANES, tm, LANES), F32)],
        compiler_params=_cparams("parallel"),
        name="in_proj",
    )(x2, norm_w.reshape(1, D_MODEL), *w_parts)


def _s5_prep_kernel(*refs):
    for gi in range(refs[0].shape[0]):
        _s5_prep_group(*[r.at[gi:gi + 1] for r in refs])


def _s5_prep_group(lr_ref, li_ref, logdt_ref, bt_ref, bti_ref, btr_t_ref, bti_t_ref, cr_ref, ci_ref, d_ref,
                   tt_ref, w_ref, mt_ref, pq_ref):
    p64 = S5_STATE
    lr, li = lr_ref[0], li_ref[0]
    dt = jnp.exp(logdt_ref[0])
    mag = jnp.exp(lr * dt)
    lb_re = mag * jnp.cos(li * dt)
    lb_im = mag * jnp.sin(li * dt)
    den = lr * lr + li * li
    f_re = ((lb_re - 1.0) * lr + lb_im * li) / den
    f_im = (lb_im * lr - (lb_re - 1.0) * li) / den

    pw = [(jnp.ones_like(lb_re), jnp.zeros_like(lb_im))]
    for _ in range(S5_CHUNK):
        pr, pi = pw[-1]
        pw.append((pr * lb_re - pi * lb_im, pr * lb_im + pi * lb_re))

    cr, ci = cr_ref[0], ci_ref[0]
    btr_t, bti_t = btr_t_ref[0], bti_t_ref[0]
    g_re, g_im, m_re, m_im, w_re, w_im = [], [], [], [], [], []
    for k in range(S5_CHUNK):
        pr, pi = pw[k]
        afr = pr * f_re - pi * f_im
        afi = pr * f_im + pi * f_re
        g_re.append(afr * cr - afi * ci)
        g_im.append(afr * ci + afi * cr)
        qr, qi = pw[k + 1]
        m_re.append(qr * cr - qi * ci)
        m_im.append(qr * ci + qi * cr)
        sr, si = pw[S5_CHUNK - 1 - k]
        wfr = sr * f_re - si * f_im
        wfi = sr * f_im + si * f_re
        w_re.append(wfr * btr_t - wfi * bti_t)
        w_im.append(wfr * bti_t + wfi * btr_t)
    cat = lambda blocks: jnp.concatenate(blocks, axis=0)
    w_ref[0, :, 0:p64] = cat(w_re).astype(w_ref.dtype)
    w_ref[0, :, p64:2 * p64] = cat(w_im).astype(w_ref.dtype)
    mt_ref[0, :, 0:p64] = cat(m_re).astype(mt_ref.dtype)
    mt_ref[0, :, p64:2 * p64] = (-cat(m_im)).astype(mt_ref.dtype)

    hi = lax.Precision.HIGHEST
    kk = (jnp.dot(cat(g_re), bt_ref[0], precision=hi, preferred_element_type=F32)
          - jnp.dot(cat(g_im), bti_ref[0], precision=hi, preferred_element_type=F32))
    rows = lax.broadcasted_iota(jnp.int32, (S5_K, S5_K), 0)
    cols = lax.broadcasted_iota(jnp.int32, (S5_K, S5_K), 1)
    col_blk = cols // S5_GROUP
    tt = jnp.where(rows == cols, d_ref[0], 0.0)
    for s in range(S5_CHUNK):
        if s == 0:
            shifted = kk
        else:
            shifted = jnp.concatenate([jnp.zeros((s * S5_GROUP, S5_K), F32), kk[:S5_K - s * S5_GROUP, :]], axis=0)
        tt = tt + jnp.where(col_blk == s, shifted, 0.0)
    tt_ref[0] = tt.astype(tt_ref.dtype)

    cr16, ci16 = pw[S5_CHUNK]
    pq_ref[0, 0:1, :] = jnp.concatenate([cr16, cr16], axis=1)
    pq_ref[0, 1:2, :] = jnp.concatenate([-ci16, ci16], axis=1)


def _s5_prep(a_re, a_im, b_re, b_im, c_re, c_im, d_skip, log_dt):
    g, p = S5_GROUPS, S5_STATE
    tile_lanes = lambda b: jnp.tile(b, (1, 1, S5_CHUNK))
    swap = lambda b: jnp.transpose(b, (0, 2, 1))
    args = (
        a_re.reshape(g, 1, p), a_im.reshape(g, 1, p), log_dt.reshape(g, 1, 1),
        tile_lanes(b_re), tile_lanes(b_im), swap(b_re), swap(b_im), c_re, c_im,
        jnp.tile(d_skip, (1, S5_CHUNK)).reshape(g, 1, S5_K),
    )
    spec = lambda a: pl.BlockSpec((S5_PREP_GROUPS,) + a.shape[1:], lambda i: (i, 0, 0))
    out_shapes = (
        jax.ShapeDtypeStruct((g, S5_K, S5_K), BF16),
        jax.ShapeDtypeStruct((g, S5_K, 2 * p), BF16),
        jax.ShapeDtypeStruct((g, S5_K, 2 * p), BF16),
        jax.ShapeDtypeStruct((g, 2, 2 * p), F32),
    )
    return pl.pallas_call(
        _s5_prep_kernel,
        grid=(g // S5_PREP_GROUPS,),
        in_specs=[spec(a) for a in args],
        out_specs=[spec(o) for o in out_shapes],
        out_shape=out_shapes,
        compiler_params=_cparams("parallel"),
        name="s5_prep",
    )(*args)


def _s5_kernel(ug_ref, tt_ref, w_ref, mt_ref, pq_ref, yg_ref, l_scr, s_scr, state_scr):
    ng = S5_GROUPS
    nchunks = ug_ref.shape[1]

    @pl.when(pl.program_id(1) == 0)
    def _():
        state_scr[...] = jnp.zeros_like(state_scr)

    def local(g, carry):
        l_scr[pl.ds(g, nchunks, stride=ng), :] = _dot(ug_ref[g], w_ref[g])
        return carry

    lax.fori_loop(0, ng, local, 0, unroll=8)

    p_mul = pq_ref[0]
    q_mul = pq_ref[1]

    def step(j, carry):
        x, xs = carry
        row = pl.multiple_of(j * ng, ng)
        s_scr[pl.ds(row, ng), :] = x
        l = l_scr[pl.ds(row, ng), :]
        ls = pltpu.roll(l, S5_STATE, 1)
        return p_mul * x + q_mul * xs + l, p_mul * xs - q_mul * x + ls

    x, xs = lax.fori_loop(0, nchunks, step, (state_scr[0], state_scr[1]), unroll=8)
    state_scr[0] = x
    state_scr[1] = xs

    def output(g, carry):
        u = ug_ref[g]
        s_in = s_scr[pl.ds(g, nchunks, stride=ng), :].astype(BF16)
        y = _dot_nt(u, tt_ref[g]) + _dot_nt(s_in, mt_ref[g])
        yg_ref[g] = jax.nn.gelu(y).astype(yg_ref.dtype)
        return carry

    lax.fori_loop(0, ng, output, 0, unroll=8)


def _s5_mixer(ug, bsz, mats):
    tt, w_mat, mt, pq = mats
    pq = jnp.transpose(pq, (1, 0, 2))
    g, nch, _ = ug.shape
    nch_b = nch // bsz
    cb = min(S5_CHUNK_BLOCK, nch_b)
    nblk = nch_b // cb
    blk = pl.BlockSpec((g, cb, S5_K), lambda b, j: (0, b * nblk + j, 0))
    return pl.pallas_call(
        _s5_kernel,
        grid=(bsz, nblk),
        in_specs=[blk, _const_spec(tt.shape), _const_spec(w_mat.shape), _const_spec(mt.shape), _const_spec(pq.shape)],
        out_specs=blk,
        out_shape=jax.ShapeDtypeStruct(ug.shape, BF16),
        scratch_shapes=[
            pltpu.VMEM((cb * g, 2 * S5_STATE), F32),
            pltpu.VMEM((cb * g, 2 * S5_STATE), F32),
            pltpu.VMEM((2, g, 2 * S5_STATE), F32),
        ],
        compiler_params=_cparams("arbitrary", "arbitrary"),
        name="s5",
    )(ug, tt, w_mat, mt, pq)


def _mlstm_kernel(qk_ref, tail_ref, v_ref, o_ref, small_ref, cw_ref, cb_ref, gbias_ref, y_ref, ct_scr, m_scr):
    first = pl.program_id(0) == 0
    n = qk_ref.shape[0]
    dh = ML_HEAD_DIM

    @pl.when(first)
    def _():
        ct_scr[...] = jnp.zeros_like(ct_scr)
        m_scr[...] = jnp.zeros_like(m_scr)

    ct_in = [ct_scr[h] for h in range(ML_HEADS)]
    m_in = [m_scr[h][:, 0:1] for h in range(ML_HEADS)]
    qk = _silu(_causal_conv(qk_ref, tail_ref, cw_ref, cb_ref, first))
    gate = small_ref[...] + gbias_ref[...]
    lf = jax.nn.log_sigmoid(gate)
    tri = _lower_tri(n)
    b_all = _dot3_left(tri.astype(BF16), lf)
    gate_t = gate.T
    b_t = b_all.T
    ones_col = (lax.broadcasted_iota(jnp.int32, (n, dh), 1) == 0).astype(BF16)

    outs, ct_out, m_out = [], [], []
    for h in range(ML_HEADS):
        q = qk[:, h * dh:(h + 1) * dh].astype(BF16)
        k = qk[:, ML_WIDTH + h * dh:ML_WIDTH + (h + 1) * dh] * (dh ** -0.5)
        k_t = k.T.astype(BF16)
        v_ext = jnp.concatenate([v_ref[:, h * dh:(h + 1) * dh], ones_col], axis=1)
        b_col = b_all[:, SLOT_F + h:SLOT_F + h + 1]
        ig_col = gate[:, SLOT_I + h:SLOT_I + h + 1]
        b_row = b_t[SLOT_F + h:SLOT_F + h + 1, :]
        ig_row = gate_t[SLOT_I + h:SLOT_I + h + 1, :]
        m_prev = m_in[h]
        ct_prev = ct_in[h]

        dmat = jnp.where(tri, b_col - b_row + ig_row, -jnp.inf)
        inter = b_col + m_prev
        m_t = jnp.maximum(jnp.max(dmat, axis=1, keepdims=True), inter)
        p = _dot(q, k_t) * jnp.exp(dmat - m_t)
        w_inter = jnp.exp(inter - m_t)
        num_den = _dot(p.astype(BF16), v_ext) + w_inter * _dot(q, ct_prev.astype(BF16))
        den = num_den[:, dh:dh + 1]
        denom = jnp.maximum(jnp.abs(den), jnp.exp(-m_t))
        out = num_den[:, :dh] / denom * _sigmoid(o_ref[:, h * dh:(h + 1) * dh].astype(F32))
        outs.append(out.astype(y_ref.dtype))

        b_end = b_col[n - 1:n, :]
        a_end = b_end - b_col + ig_col
        m_loc = jnp.max(a_end, axis=0, keepdims=True)
        w_end = jnp.exp(a_end - m_loc)
        m_new = jnp.maximum(b_end + m_prev, m_loc)
        s_prev = jnp.exp(b_end + m_prev - m_new)
        s_loc = jnp.exp(m_loc - m_new)
        upd = _dot(k_t, (w_end * v_ext.astype(F32)).astype(BF16))
        ct_out.append(s_prev * ct_prev + s_loc * upd)
        m_out.append(jnp.broadcast_to(m_new, m_scr.shape[1:]))

    y_ref[...] = jnp.concatenate(outs, axis=1)
    for h in range(ML_HEADS):
        ct_scr[h] = ct_out[h]
        m_scr[h] = m_out[h]


def _seq_specs(n, bsz):
    row = lambda w: pl.BlockSpec((bsz, n, w), lambda c: (0, c, 0))
    tail = lambda w: pl.BlockSpec((bsz, TAIL, w), lambda c: (0, jnp.maximum(c * (n // TAIL) - 1, 0), 0))
    return row, tail


def _by_seq(a, bsz):
    return a.reshape(bsz, a.shape[0] // bsz, a.shape[1])


def _mlstm_call_parts(qk, v, o, small, conv_w, conv_b, gbias, n, bsz):
    row, tail = _seq_specs(n, bsz)
    qk, v, o, small = (_by_seq(a, bsz) for a in (qk, v, o, small))
    in_specs = [row(2 * ML_WIDTH), tail(2 * ML_WIDTH), row(ML_WIDTH), row(ML_WIDTH), row(LANES),
                _const_spec((CONV_K, 2 * ML_WIDTH)), _const_spec((1, 2 * ML_WIDTH)), _const_spec((1, LANES))]
    args = (qk, qk, v, o, small, conv_w, conv_b.reshape(1, -1), gbias)
    scratch = [pltpu.VMEM((bsz, ML_HEADS, ML_HEAD_DIM, 2 * ML_HEAD_DIM), F32),
               pltpu.VMEM((bsz, ML_HEADS, 1, LANES), F32)]
    return in_specs, args, 5, row(ML_WIDTH), jax.ShapeDtypeStruct(qk.shape[:2] + (ML_WIDTH,), BF16), scratch


def _ssd_kernel(z_ref, xbc_ref, tail_ref, small_ref, cw_ref, cb_ref, gbias_ref, alog_ref, dskip_ref, normw_ref, expand_ref,
                y_ref, state_scr):
    first = pl.program_id(0) == 0
    n = z_ref.shape[0]
    gw = M2_GROUP_WIDTH
    hd = M2_HEAD_DIM

    @pl.when(first)
    def _():
        state_scr[...] = jnp.zeros_like(state_scr)

    state_in = [state_scr[g] for g in range(M2_GROUPS)]
    xbc = _silu(_causal_conv(xbc_ref, tail_ref, cw_ref, cb_ref, first))
    xs = xbc[:, :M2_WIDTH]
    lane = lax.broadcasted_iota(jnp.int32, (1, LANES), 1)
    dt_lanes = (lane >= SLOT_DT) & (lane < SLOT_DT + M2_HEADS)
    dt = jnp.where(dt_lanes, jax.nn.softplus(small_ref[...] + gbias_ref[...]), 0.0)
    a_row = jnp.where(dt_lanes, -jnp.exp(alog_ref[...]), 0.0)
    tri = _lower_tri(n)
    cum = _dot3_left(tri.astype(BF16), dt * a_row)
    cum_t = cum.T
    cum_end = cum[n - 1:n, :]
    expand = expand_ref[...]
    dt_e = _dot(dt.astype(BF16), expand)
    decay_in = _dot(jnp.exp(cum).astype(BF16), expand)
    decay_out = _dot((dt * jnp.exp(cum_end - cum)).astype(BF16), expand)
    decay_chunk = _dot3_right(jnp.exp(cum[n - SUBLANES:n, :]), expand)[SUBLANES - 1:SUBLANES, :]
    xdt_b = (xs * dt_e).astype(BF16)
    x_in = (xs * decay_out).astype(BF16)
    left_half = lax.broadcasted_iota(jnp.int32, (n, 2 * hd), 1) < hd

    y_out, state_out = [], []
    for g in range(M2_GROUPS):
        b_g = xbc[:, M2_WIDTH + g * M2_STATE:M2_WIDTH + (g + 1) * M2_STATE]
        c_g = xbc[:, M2_WIDTH + (M2_GROUPS + g) * M2_STATE:M2_WIDTH + (M2_GROUPS + g + 1) * M2_STATE].astype(BF16)
        b_t = b_g.T.astype(BF16)
        cb = _dot(c_g, b_t)
        y_diag = []
        for pair in range(M2_HPG // 2):
            head = g * M2_HPG + 2 * pair
            ws = []
            for hh in (head, head + 1):
                cum_col = cum[:, SLOT_DT + hh:SLOT_DT + hh + 1]
                cum_row = cum_t[SLOT_DT + hh:SLOT_DT + hh + 1, :]
                ws.append((cb * jnp.exp(jnp.where(tri, cum_col - cum_row, -jnp.inf))).astype(BF16))
            xp = xdt_b[:, head * hd:(head + 2) * hd]
            zero = jnp.zeros_like(xp)
            rhs = jnp.concatenate([jnp.where(left_half, xp, zero), jnp.where(left_half, zero, xp)], axis=0)
            y_diag.append(_dot(jnp.concatenate(ws, axis=1), rhs))
        cols = slice(g * gw, (g + 1) * gw)
        st = state_in[g]
        y_off = _dot(c_g, st.astype(BF16)) * decay_in[:, cols]
        y = jnp.concatenate(y_diag, axis=1) + y_off + dskip_ref[:, cols] * xs[:, cols]
        y = y * _silu(z_ref[:, cols].astype(F32))
        y = y * lax.rsqrt(jnp.mean(y * y, axis=-1, keepdims=True) + EPS)
        y_out.append((y * normw_ref[:, cols]).astype(y_ref.dtype))
        state_out.append(st * decay_chunk[:, cols] + _dot(b_t, x_in[:, cols]))

    y_ref[...] = jnp.concatenate(y_out, axis=1)
    for g in range(M2_GROUPS):
        state_scr[g] = state_out[g]


def _ssd_call_parts(z, xbc, small, conv_w, conv_b, gbias, alog_row, dskip_row, norm_w, n, bsz):
    row, tail = _seq_specs(n, bsz)
    z, xbc, small = (_by_seq(a, bsz) for a in (z, xbc, small))
    expand = np.zeros((LANES, M2_WIDTH), np.float32)
    for h in range(M2_HEADS):
        expand[SLOT_DT + h, h * M2_HEAD_DIM:(h + 1) * M2_HEAD_DIM] = 1.0
    in_specs = [row(M2_WIDTH), row(M2_CONV_DIM), tail(M2_CONV_DIM), row(LANES),
                _const_spec((CONV_K, M2_CONV_DIM)), _const_spec((1, M2_CONV_DIM)), _const_spec((1, LANES)), _const_spec((1, LANES)),
                _const_spec((1, M2_WIDTH)), _const_spec((1, M2_WIDTH)), _const_spec((LANES, M2_WIDTH))]
    args = (z, xbc, xbc, small, conv_w, conv_b.reshape(1, -1), gbias, alog_row, dskip_row, norm_w.reshape(1, -1),
            jnp.asarray(expand, BF16))
    scratch = [pltpu.VMEM((bsz, M2_GROUPS, M2_STATE, M2_GROUP_WIDTH), F32)]
    return in_specs, args, 4, row(M2_WIDTH), jax.ShapeDtypeStruct(z.shape[:2] + (M2_WIDTH,), BF16), scratch


def _mixers(ml_inputs, ssd_inputs, bsz):
    t = ml_inputs[0].shape[0]
    seq = t // bsz
    n = min(SEQ_CHUNK, seq)
    ml_specs, ml_args, ml_nseq, ml_out_spec, ml_out_shape, ml_scratch = _mlstm_call_parts(*ml_inputs, n, bsz)
    ssd_specs, ssd_args, ssd_nseq, ssd_out_spec, ssd_out_shape, ssd_scratch = _ssd_call_parts(*ssd_inputs, n, bsz)
    n_ml, n_ssd = len(ml_args), len(ssd_args)

    def body(*refs):
        ins, outs, scr = refs[:n_ml + n_ssd], refs[n_ml + n_ssd:n_ml + n_ssd + 2], refs[n_ml + n_ssd + 2:]
        for b in range(bsz):
            per_seq = lambda rs, k: [r.at[b] if i < k else r for i, r in enumerate(rs)]
            _mlstm_kernel(*per_seq(ins[:n_ml], ml_nseq), outs[0].at[b], *[s.at[b] for s in scr[:len(ml_scratch)]])
            _ssd_kernel(*per_seq(ins[n_ml:], ssd_nseq), outs[1].at[b], *[s.at[b] for s in scr[len(ml_scratch):]])

    yb, yc = pl.pallas_call(
        body,
        grid=(seq // n,),
        in_specs=ml_specs + ssd_specs,
        out_specs=[ml_out_spec, ssd_out_spec],
        out_shape=[ml_out_shape, ssd_out_shape],
        scratch_shapes=ml_scratch + ssd_scratch,
        compiler_params=_cparams("arbitrary"),
        name="mixers",
    )(*ml_args, *ssd_args)
    return yb.reshape(t, ML_WIDTH), yc.reshape(t, M2_WIDTH)


def _merge_kernel(x_ref, yg_ref, yb_ref, yc_ref, g_ref, wglu_ref, wa_ref, wb_ref, wc_ref, wout_ref, nmp_ref, out_ref, ys_scr):
    d = D_MODEL
    gate = lambda i: _sigmoid(g_ref[:, i * d:(i + 1) * d].astype(F32))
    merged_bc = gate(1) * _dot(yb_ref[...], wb_ref[...]) + gate(2) * _dot(yc_ref[...], wc_ref[...])
    nchunks = yg_ref.shape[1]
    for jb in range(S5_GROUPS // GROUPS_PER_VREG):
        for half in range(S5_CHUNK // GROUPS_PER_VREG):
            v = [yg_ref[jb * GROUPS_PER_VREG + g, :, half * LANES:(half + 1) * LANES].astype(F32) for g in range(GROUPS_PER_VREG)]
            v = _block_transpose(v)
            for t in range(GROUPS_PER_VREG):
                ys_scr[jb, pl.ds(half * GROUPS_PER_VREG + t, nchunks, stride=S5_CHUNK), :] = v[t]
    ys5 = jnp.concatenate([ys_scr[jb] for jb in range(S5_WIDTH // LANES)], axis=1)
    ya = ys5 * _sigmoid(_dot(ys5.astype(BF16), wglu_ref[...]))
    merged = gate(0) * _dot(ya.astype(BF16), wa_ref[...]) + merged_bc
    mix = _dot(merged.astype(BF16), wout_ref[...])
    out_ref[...] = x_ref[...] + _rms(mix, nmp_ref[...])


def _merge(x2, yg, yb, yc, gates, weights, norm_post, layer):
    t = x2.shape[0]
    tm = min(TOKEN_BLOCK, t)
    row = lambda w: pl.BlockSpec((tm, w), lambda i: (i, 0))
    return pl.pallas_call(
        _merge_kernel,
        grid=(t // tm,),
        in_specs=[row(D_MODEL), pl.BlockSpec((S5_GROUPS, tm // S5_CHUNK, S5_K), lambda i: (0, i, 0)),
                  row(ML_WIDTH), row(M2_WIDTH), row(N_BRANCH * D_MODEL)]
        + [_layer_spec(w, layer) for w in weights] + [_const_spec((1, D_MODEL))],
        out_specs=row(D_MODEL),
        out_shape=jax.ShapeDtypeStruct((t, D_MODEL), F32),
        scratch_shapes=[pltpu.VMEM((S5_WIDTH // LANES, tm, LANES), F32)],
        compiler_params=_cparams("parallel"),
        name="merge",
    )(x2, yg, yb, yc, gates, *weights, norm_post.reshape(1, D_MODEL))


def _ffn_kernel(x_ref, npre_ref, npost_ref, wg_ref, wu_ref, wd_ref, out_ref):
    x = x_ref[...]
    hb = _rms(x, npre_ref[...]).astype(BF16)
    act = _silu(_dot(hb, wg_ref[...])) * _dot(hb, wu_ref[...])
    ffn = _dot(act.astype(BF16), wd_ref[...])
    out_ref[...] = x + _rms(ffn, npost_ref[...])


def _ffn(x2, weights, norm_pre, norm_post, layer):
    t = x2.shape[0]
    tm = min(TOKEN_BLOCK, t)
    row = pl.BlockSpec((tm, D_MODEL), lambda i: (i, 0))
    return pl.pallas_call(
        _ffn_kernel,
        grid=(t // tm,),
        in_specs=[row, _const_spec((1, D_MODEL)), _const_spec((1, D_MODEL))] + [_layer_spec(w, layer) for w in weights],
        out_specs=row,
        out_shape=jax.ShapeDtypeStruct((t, D_MODEL), F32),
        compiler_params=_cparams("parallel"),
        name="ffn",
    )(x2, norm_pre.reshape(1, D_MODEL), norm_post.reshape(1, D_MODEL), *weights)


def _gate_bias_row(bias_i, bias_f, dt_bias):
    pad = jnp.zeros((LANES - 2 * ML_HEADS - M2_HEADS,), F32)
    return jnp.concatenate([bias_i, bias_f, dt_bias, pad]).reshape(1, LANES)


def _head_lane_row(v):
    pad_l = jnp.zeros((SLOT_DT,), F32)
    pad_r = jnp.zeros((LANES - SLOT_DT - M2_HEADS,), F32)
    return jnp.concatenate([pad_l, v, pad_r]).reshape(1, LANES)


def _layer(x2, bsz, p, big, layer):
    ug, qk, v, o, z, xbc, gates, small = _in_proj(x2, p["norm_mix_pre"], big["w_in"], layer)
    mats = _s5_prep(p["s5_a_re"], p["s5_a_im"], p["s5_b_re"], p["s5_b_im"], p["s5_c_re"], p["s5_c_im"], p["s5_d"], p["s5_log_dt"])
    yg = _s5_mixer(ug, bsz, mats)
    gbias = _gate_bias_row(p["ml_bias_i"], p["ml_bias_f"], p["m2_dt_bias"])
    yb, yc = _mixers(
        (qk, v, o, small, p["ml_conv_w"], p["ml_conv_b"], gbias),
        (z, xbc, small, p["m2_conv_w"], p["m2_conv_b"], gbias, _head_lane_row(p["m2_a_log"]),
         jnp.repeat(p["m2_d"], M2_HEAD_DIM).reshape(1, M2_WIDTH), p["m2_norm_w"]),
        bsz)
    x1 = _merge(x2, yg, yb, yc, gates, big["merge"], p["norm_mix_post"], layer)
    return _ffn(x1, big["ffn"], p["norm_ffn_pre"], p["norm_ffn_post"], layer)


_PARAM_NAMES = ("norm_mix_pre", "norm_mix_post", "w_in", "s5_a_re", "s5_a_im", "s5_b_re", "s5_b_im", "s5_c_re", "s5_c_im", "s5_d",
                "s5_log_dt", "s5_w_glu", "ml_conv_w", "ml_conv_b", "ml_bias_i", "ml_bias_f", "m2_conv_w", "m2_conv_b", "m2_dt_bias",
                "m2_a_log", "m2_d", "m2_norm_w", "w_br_a", "w_br_b", "w_br_c", "w_out", "norm_ffn_pre", "norm_ffn_post",
                "w_ffn_gate", "w_ffn_up", "w_ffn_down")


def kernel(x, norm_mix_pre, norm_mix_post, w_in, s5_a_re, s5_a_im, s5_b_re, s5_b_im, s5_c_re, s5_c_im, s5_d, s5_log_dt, s5_w_glu, ml_conv_w, ml_conv_b, ml_bias_i, ml_bias_f, m2_conv_w, m2_conv_b, m2_dt_bias, m2_a_log, m2_d, m2_norm_w, w_br_a, w_br_b, w_br_c, w_out, norm_ffn_pre, norm_ffn_post, w_ffn_gate, w_ffn_up, w_ffn_down):
    stacked = (norm_mix_pre, norm_mix_post, w_in, s5_a_re, s5_a_im, s5_b_re, s5_b_im, s5_c_re, s5_c_im, s5_d, s5_log_dt, s5_w_glu,
               ml_conv_w, ml_conv_b, ml_bias_i, ml_bias_f, m2_conv_w, m2_conv_b, m2_dt_bias, m2_a_log, m2_d, m2_norm_w,
               w_br_a, w_br_b, w_br_c, w_out, norm_ffn_pre, norm_ffn_post, w_ffn_gate, w_ffn_up, w_ffn_down)
    bsz, seq, d = x.shape
    x2 = x.reshape(bsz * seq, d)
    big = {
        "w_in": _split_w_in(w_in),
        "merge": [w.astype(BF16) for w in (s5_w_glu, w_br_a, w_br_b, w_br_c, w_out)],
        "ffn": [w.astype(BF16) for w in (w_ffn_gate, w_ffn_up, w_ffn_down)],
    }
    for layer in range(norm_mix_pre.shape[0]):
        x2 = _layer(x2, bsz, {name: arr[layer] for name, arr in zip(_PARAM_NAMES, stacked)}, big, layer)
    return x2.reshape(bsz, seq, d)
```

```python
import jax
import jax.numpy as jnp
import numpy as np
from jax import lax
from jax.experimental import pallas as pl
from jax.experimental.pallas import tpu as pltpu

F32 = jnp.float32
BF16 = jnp.bfloat16

D_MODEL = 1024
EPS = 1e-6
CONV_K = 4

S5_WIDTH = 512
S5_GROUP = 16
S5_GROUPS = S5_WIDTH // S5_GROUP
S5_STATE = 64
S5_CHUNK = 16
S5_K = S5_CHUNK * S5_GROUP
S5_CHUNK_BLOCK = 256
S5_PREP_GROUPS = 4

ML_WIDTH = 512
ML_HEADS = 4
ML_HEAD_DIM = ML_WIDTH // ML_HEADS

M2_WIDTH = 1024
M2_HEAD_DIM = 64
M2_HEADS = M2_WIDTH // M2_HEAD_DIM
M2_GROUPS = 2
M2_HPG = M2_HEADS // M2_GROUPS
M2_STATE = 128
M2_CONV_DIM = M2_WIDTH + 2 * M2_GROUPS * M2_STATE
M2_GROUP_WIDTH = M2_WIDTH // M2_GROUPS

N_BRANCH = 3
FFN_HIDDEN = -(-(8 * D_MODEL) // (3 * 256)) * 256

LANES = 128
SUBLANES = 8
TAIL = SUBLANES
GROUPS_PER_VREG = LANES // S5_GROUP

SLOT_I = 0
SLOT_F = ML_HEADS
SLOT_DT = 2 * ML_HEADS

SEQ_CHUNK = 256
TOKEN_BLOCK = 512

VMEM_LIMIT = 56 * 1024 * 1024


def _cparams(*sem):
    return pltpu.CompilerParams(dimension_semantics=sem, vmem_limit_bytes=VMEM_LIMIT)


def _const_spec(shape):
    nd = len(shape)
    return pl.BlockSpec(shape, lambda *_: (0,) * nd, pipeline_mode=pl.Buffered(1))


def _layer_spec(stacked, layer):
    shape = stacked.shape[1:]
    zeros = (0,) * len(shape)
    return pl.BlockSpec((None,) + shape, lambda *_: (layer,) + zeros, pipeline_mode=pl.Buffered(1))


def _rms(x, w):
    return x * lax.rsqrt(jnp.mean(x * x, axis=-1, keepdims=True) + EPS) * w


def _split3(x):
    hi = x.astype(BF16)
    r1 = x - hi.astype(F32)
    mid = r1.astype(BF16)
    lo = (r1 - mid.astype(F32)).astype(BF16)
    return hi, mid, lo


def _dot(a, b):
    return jnp.dot(a, b, preferred_element_type=F32)


def _dot_nt(a, b):
    return lax.dot_general(a, b, (((1,), (1,)), ((), ())), preferred_element_type=F32)


def _dot3_left(m_bf16, x):
    hi, mid, lo = _split3(x)
    return _dot(m_bf16, hi) + _dot(m_bf16, mid) + _dot(m_bf16, lo)


def _dot3_right(x, m_bf16):
    hi, mid, lo = _split3(x)
    return _dot(hi, m_bf16) + _dot(mid, m_bf16) + _dot(lo, m_bf16)


def _lower_tri(n):
    r = lax.broadcasted_iota(jnp.int32, (n, n), 0)
    c = lax.broadcasted_iota(jnp.int32, (n, n), 1)
    return r >= c


def _block_transpose(v):
    lane_blk = lax.broadcasted_iota(jnp.int32, v[0].shape, 1) // S5_GROUP
    b = GROUPS_PER_VREG // 2
    while b:
        bit_set = (lane_blk & b) != 0
        out = list(v)
        for r in range(GROUPS_PER_VREG):
            if r & b == 0:
                rp = r | b
                out[r] = jnp.where(bit_set, pltpu.roll(v[rp], b * S5_GROUP, 1), v[r])
                out[rp] = jnp.where(bit_set, v[rp], pltpu.roll(v[r], LANES - b * S5_GROUP, 1))
        v = out
        b //= 2
    return v


IN_SEGMENTS = (
    ("qk", 2 * ML_WIDTH, BF16),
    ("v", ML_WIDTH, BF16),
    ("o", ML_WIDTH, BF16),
    ("z", M2_WIDTH, BF16),
    ("xbc", M2_CONV_DIM, BF16),
    ("gates", N_BRANCH * D_MODEL, BF16),
    ("small", LANES, F32),
)


W_A_WIDTH = S5_WIDTH + 4 * ML_WIDTH


def _split_w_in(w_in):
    splits = (S5_WIDTH, ML_WIDTH, ML_WIDTH, ML_WIDTH, ML_WIDTH, ML_HEADS, ML_HEADS, M2_WIDTH, M2_CONV_DIM, M2_HEADS, N_BRANCH * D_MODEL)
    offs = np.cumsum((0,) + splits)
    w16 = w_in.astype(BF16)
    col = lambda n0, n1: w16[..., offs[n0]:offs[n1]]
    pad = jnp.zeros(w_in.shape[:-1] + (LANES - 2 * ML_HEADS - M2_HEADS,), BF16)
    small = jnp.concatenate([col(5, 7), col(9, 10), pad], axis=-1)
    return [w16, col(7, 8), col(8, 9), col(10, 11), small]


def _sigmoid(x):
    return 0.5 * jnp.tanh(0.5 * x) + 0.5


def _silu(x):
    h = 0.5 * x
    return h + h * jnp.tanh(h)


def _causal_conv(x_ref, tail_ref, w_ref, b_ref, first):
    n = x_ref.shape[0]
    x = x_ref[...].astype(F32)
    row = lax.broadcasted_iota(jnp.int32, (TAIL, 1), 0)
    shifts = [CONV_K - 1 - j for j in range(CONV_K - 1)]
    prev = jnp.where(first, 0.0, tail_ref[...].astype(F32))
    prev_rot = [pltpu.roll(prev, s, 0) for s in shifts]
    tiles = []
    for i in range(n // TAIL):
        cur = x[i * TAIL:(i + 1) * TAIL, :]
        cur_rot = [pltpu.roll(cur, s, 0) for s in shifts]
        acc = b_ref[...] + w_ref[CONV_K - 1:CONV_K, :] * cur
        for j, s in enumerate(shifts):
            acc = acc + w_ref[j:j + 1, :] * jnp.where(row < s, prev_rot[j], cur_rot[j])
        tiles.append(acc)
        prev_rot = cur_rot
    return jnp.concatenate(tiles, axis=0)


def _in_proj_kernel(x_ref, nw_ref, wa_ref, wz_ref, wxbc_ref, wg_ref, wsmall_ref,
                    ug_ref, qk_ref, v_ref, o_ref, z_ref, xbc_ref, g_ref, small_ref, u_scr):
    hb = _rms(x_ref[...], nw_ref[...]).astype(BF16)
    u = _dot(hb, wa_ref[:, 0:S5_WIDTH])
    off = S5_WIDTH
    for ref in (qk_ref, v_ref, o_ref):
        width = ref.shape[1]
        ref[...] = _dot(hb, wa_ref[:, off:off + width]).astype(ref.dtype)
        off += width
    for ref, w_ref in ((z_ref, wz_ref), (xbc_ref, wxbc_ref), (g_ref, wg_ref), (small_ref, wsmall_ref)):
        ref[...] = _dot(hb, w_ref[...]).astype(ref.dtype)

    nchunks = u_scr.shape[1] // S5_CHUNK
    for jb in range(S5_GROUPS // GROUPS_PER_VREG):
        u_scr[jb] = u[:, jb * LANES:(jb + 1) * LANES]
        for half in range(S5_CHUNK // GROUPS_PER_VREG):
            v = [u_scr[jb, pl.ds(half * GROUPS_PER_VREG + t, nchunks, stride=S5_CHUNK), :] for t in range(GROUPS_PER_VREG)]
            v = _block_transpose(v)
            for g in range(GROUPS_PER_VREG):
                ug_ref[jb * GROUPS_PER_VREG + g, :, half * LANES:(half + 1) * LANES] = v[g].astype(ug_ref.dtype)


def _in_proj(x2, norm_w, w_parts, layer):
    t = x2.shape[0]
    tm = min(TOKEN_BLOCK, t)
    nchunks = tm // S5_CHUNK
    return pl.pallas_call(
        _in_proj_kernel,
        grid=(t // tm,),
        in_specs=[pl.BlockSpec((tm, D_MODEL), lambda i: (i, 0)), _const_spec((1, D_MODEL)),
                  pl.BlockSpec((None, D_MODEL, W_A_WIDTH), lambda i: (layer, 0, 0), pipeline_mode=pl.Buffered(1))]
        + [_layer_spec(w, layer) for w in w_parts[1:]],
        out_specs=[pl.BlockSpec((S5_GROUPS, nchunks, S5_K), lambda i: (0, i, 0))]
        + [pl.BlockSpec((tm, w), lambda i: (i, 0)) for _, w, _ in IN_SEGMENTS],
        out_shape=[jax.ShapeDtypeStruct((S5_GROUPS, t // S5_CHUNK, S5_K), BF16)]
        + [jax.ShapeDtypeStruct((t, w), dt) for _, w, dt in IN_SEGMENTS],
        scratch_shapes=[pltpu.VMEM((S5_WIDTH // LANES, tm, LANES), F32)],
        compiler_params=_cparams("parallel"),
        name="in_proj",
    )(x2, norm_w.reshape(1, D_MODEL), *w_parts)


def _s5_prep_kernel(*refs):
    for gi in range(refs[0].shape[0]):
        _s5_prep_group(*[r.at[gi:gi + 1] for r in refs])


def _s5_prep_group(lr_ref, li_ref, logdt_ref, bt_ref, bti_ref, btr_t_ref, bti_t_ref, cr_ref, ci_ref, d_ref,
                   tt_ref, w_ref, mt_ref, pq_ref):
    p64 = S5_STATE
    lr, li = lr_ref[0], li_ref[0]
    dt = jnp.exp(logdt_ref[0])
    mag = jnp.exp(lr * dt)
    lb_re = mag * jnp.cos(li * dt)
    lb_im = mag * jnp.sin(li * dt)
    den = lr * lr + li * li
    f_re = ((lb_re - 1.0) * lr + lb_im * li) / den
    f_im = (lb_im * lr - (lb_re - 1.0) * li) / den

    pw = [(jnp.ones_like(lb_re), jnp.zeros_like(lb_im))]
    for _ in range(S5_CHUNK):
        pr, pi = pw[-1]
        pw.append((pr * lb_re - pi * lb_im, pr * lb_im + pi * lb_re))

    cr, ci = cr_ref[0], ci_ref[0]
    btr_t, bti_t = btr_t_ref[0], bti_t_ref[0]
    g_re, g_im, m_re, m_im, w_re, w_im = [], [], [], [], [], []
    for k in range(S5_CHUNK):
        pr, pi = pw[k]
        afr = pr * f_re - pi * f_im
        afi = pr * f_im + pi * f_re
        g_re.append(afr * cr - afi * ci)
        g_im.append(afr * ci + afi * cr)
        qr, qi = pw[k + 1]
        m_re.append(qr * cr - qi * ci)
        m_im.append(qr * ci + qi * cr)
        sr, si = pw[S5_CHUNK - 1 - k]
        wfr = sr * f_re - si * f_im
        wfi = sr * f_im + si * f_re
        w_re.append(wfr * btr_t - wfi * bti_t)
        w_im.append(wfr * bti_t + wfi * btr_t)
    cat = lambda blocks: jnp.concatenate(blocks, axis=0)
    w_ref[0, :, 0:p64] = cat(w_re).astype(w_ref.dtype)
    w_ref[0, :, p64:2 * p64] = cat(w_im).astype(w_ref.dtype)
    mt_ref[0, :, 0:p64] = cat(m_re).astype(mt_ref.dtype)
    mt_ref[0, :, p64:2 * p64] = (-cat(m_im)).astype(mt_ref.dtype)

    hi = lax.Precision.HIGHEST
    kk = (jnp.dot(cat(g_re), bt_ref[0], precision=hi, preferred_element_type=F32)
          - jnp.dot(cat(g_im), bti_ref[0], precision=hi, preferred_element_type=F32))
    rows = lax.broadcasted_iota(jnp.int32, (S5_K, S5_K), 0)
    cols = lax.broadcasted_iota(jnp.int32, (S5_K, S5_K), 1)
    col_blk = cols // S5_GROUP
    tt = jnp.where(rows == cols, d_ref[0], 0.0)
    for s in range(S5_CHUNK):
        if s == 0:
            shifted = kk
        else:
            shifted = jnp.concatenate([jnp.zeros((s * S5_GROUP, S5_K), F32), kk[:S5_K - s * S5_GROUP, :]], axis=0)
        tt = tt + jnp.where(col_blk == s, shifted, 0.0)
    tt_ref[0] = tt.astype(tt_ref.dtype)

    cr16, ci16 = pw[S5_CHUNK]
    pq_ref[0, 0:1, :] = jnp.concatenate([cr16, cr16], axis=1)
    pq_ref[0, 1:2, :] = jnp.concatenate([-ci16, ci16], axis=1)


def _s5_prep(a_re, a_im, b_re, b_im, c_re, c_im, d_skip, log_dt):
    g, p = S5_GROUPS, S5_STATE
    tile_lanes = lambda b: jnp.tile(b, (1, 1, S5_CHUNK))
    swap = lambda b: jnp.transpose(b, (0, 2, 1))
    args = (
        a_re.reshape(g, 1, p), a_im.reshape(g, 1, p), log_dt.reshape(g, 1, 1),
        tile_lanes(b_re), tile_lanes(b_im), swap(b_re), swap(b_im), c_re, c_im,
        jnp.tile(d_skip, (1, S5_CHUNK)).reshape(g, 1, S5_K),
    )
    spec = lambda a: pl.BlockSpec((S5_PREP_GROUPS,) + a.shape[1:], lambda i: (i, 0, 0))
    out_shapes = (
        jax.ShapeDtypeStruct((g, S5_K, S5_K), BF16),
        jax.ShapeDtypeStruct((g, S5_K, 2 * p), BF16),
        jax.ShapeDtypeStruct((g, S5_K, 2 * p), BF16),
        jax.ShapeDtypeStruct((g, 2, 2 * p), F32),
    )
    return pl.pallas_call(
        _s5_prep_kernel,
        grid=(g // S5_PREP_GROUPS,),
        in_specs=[spec(a) for a in args],
        out_specs=[spec(o) for o in out_shapes],
        out_shape=out_shapes,
        compiler_params=_cparams("parallel"),
        name="s5_prep",
    )(*args)


def _s5_kernel(ug_ref, tt_ref, w_ref, mt_ref, pq_ref, yg_ref, l_scr, s_scr, state_scr):
    ng = S5_GROUPS
    nchunks = ug_ref.shape[1]

    @pl.when(pl.program_id(1) == 0)
    def _():
        state_scr[...] = jnp.zeros_like(state_scr)

    def local(g, carry):
        l_scr[pl.ds(g, nchunks, stride=ng), :] = _dot(ug_ref[g], w_ref[g])
        return carry

    lax.fori_loop(0, ng, local, 0, unroll=8)

    p_mul = pq_ref[0]
    q_mul = pq_ref[1]

    def step(j, carry):
        x, xs = carry
        row = pl.multiple_of(j * ng, ng)
        s_scr[pl.ds(row, ng), :] = x
        l = l_scr[pl.ds(row, ng), :]
        ls = pltpu.roll(l, S5_STATE, 1)
        return p_mul * x + q_mul * xs + l, p_mul * xs - q_mul * x + ls

    x, xs = lax.fori_loop(0, nchunks, step, (state_scr[0], state_scr[1]), unroll=8)
    state_scr[0] = x
    state_scr[1] = xs

    def output(g, carry):
        u = ug_ref[g]
        s_in = s_scr[pl.ds(g, nchunks, stride=ng), :].astype(BF16)
        y = _dot_nt(u, tt_ref[g]) + _dot_nt(s_in, mt_ref[g])
        yg_ref[g] = jax.nn.gelu(y).astype(yg_ref.dtype)
        return carry

    lax.fori_loop(0, ng, output, 0, unroll=8)


def _s5_mixer(ug, bsz, mats):
    tt, w_mat, mt, pq = mats
    pq = jnp.transpose(pq, (1, 0, 2))
    g, nch, _ = ug.shape
    nch_b = nch // bsz
    cb = min(S5_CHUNK_BLOCK, nch_b)
    nblk = nch_b // cb
    blk = pl.BlockSpec((g, cb, S5_K), lambda b, j: (0, b * nblk + j, 0))
    return pl.pallas_call(
        _s5_kernel,
        grid=(bsz, nblk),
        in_specs=[blk, _const_spec(tt.shape), _const_spec(w_mat.shape), _const_spec(mt.shape), _const_spec(pq.shape)],
        out_specs=blk,
        out_shape=jax.ShapeDtypeStruct(ug.shape, BF16),
        scratch_shapes=[
            pltpu.VMEM((cb * g, 2 * S5_STATE), F32),
            pltpu.VMEM((cb * g, 2 * S5_STATE), F32),
            pltpu.VMEM((2, g, 2 * S5_STATE), F32),
        ],
        compiler_params=_cparams("arbitrary", "arbitrary"),
        name="s5",
    )(ug, tt, w_mat, mt, pq)


def _mlstm_kernel(qk_ref, tail_ref, v_ref, o_ref, small_ref, cw_ref, cb_ref, gbias_ref, y_ref, ct_scr, m_scr):
    first = pl.program_id(0) == 0
    n = qk_ref.shape[0]
    dh = ML_HEAD_DIM

    @pl.when(first)
    def _():
        ct_scr[...] = jnp.zeros_like(ct_scr)
        m_scr[...] = jnp.zeros_like(m_scr)

    ct_in = [ct_scr[h] for h in range(ML_HEADS)]
    m_in = [m_scr[h][:, 0:1] for h in range(ML_HEADS)]
    qk = _silu(_causal_conv(qk_ref, tail_ref, cw_ref, cb_ref, first))
    gate = small_ref[...] + gbias_ref[...]
    lf = jax.nn.log_sigmoid(gate)
    tri = _lower_tri(n)
    b_all = _dot3_left(tri.astype(BF16), lf)
    gate_t = gate.T
    b_t = b_all.T
    ones_col = (lax.broadcasted_iota(jnp.int32, (n, dh), 1) == 0).astype(BF16)

    outs, ct_out, m_out = [], [], []
    for h in range(ML_HEADS):
        q = qk[:, h * dh:(h + 1) * dh].astype(BF16)
        k = qk[:, ML_WIDTH + h * dh:ML_WIDTH + (h + 1) * dh] * (dh ** -0.5)
        k_t = k.T.astype(BF16)
        v_ext = jnp.concatenate([v_ref[:, h * dh:(h + 1) * dh], ones_col], axis=1)
        b_col = b_all[:, SLOT_F + h:SLOT_F + h + 1]
        ig_col = gate[:, SLOT_I + h:SLOT_I + h + 1]
        b_row = b_t[SLOT_F + h:SLOT_F + h + 1, :]
        ig_row = gate_t[SLOT_I + h:SLOT_I + h + 1, :]
        m_prev = m_in[h]
        ct_prev = ct_in[h]

        dmat = jnp.where(tri, b_col - b_row + ig_row, -jnp.inf)
        inter = b_col + m_prev
        m_t = jnp.maximum(jnp.max(dmat, axis=1, keepdims=True), inter)
        p = _dot(q, k_t) * jnp.exp(dmat - m_t)
        w_inter = jnp.exp(inter - m_t)
        num_den = _dot(p.astype(BF16), v_ext) + w_inter * _dot(q, ct_prev.astype(BF16))
        den = num_den[:, dh:dh + 1]
        denom = jnp.maximum(jnp.abs(den), jnp.exp(-m_t))
        out = num_den[:, :dh] / denom * _sigmoid(o_ref[:, h * dh:(h + 1) * dh].astype(F32))
        outs.append(out.astype(y_ref.dtype))

        b_end = b_col[n - 1:n, :]
        a_end = b_end - b_col + ig_col
        m_loc = jnp.max(a_end, axis=0, keepdims=True)
        w_end = jnp.exp(a_end - m_loc)
        m_new = jnp.maximum(b_end + m_prev, m_loc)
        s_prev = jnp.exp(b_end + m_prev - m_new)
        s_loc = jnp.exp(m_loc - m_new)
        upd = _dot(k_t, (w_end * v_ext.astype(F32)).astype(BF16))
        ct_out.append(s_prev * ct_prev + s_loc * upd)
        m_out.append(jnp.broadcast_to(m_new, m_scr.shape[1:]))

    y_ref[...] = jnp.concatenate(outs, axis=1)
    for h in range(ML_HEADS):
        ct_scr[h] = ct_out[h]
        m_scr[h] = m_out[h]


def _seq_specs(n, bsz):
    row = lambda w: pl.BlockSpec((bsz, n, w), lambda c: (0, c, 0))
    tail = lambda w: pl.BlockSpec((bsz, TAIL, w), lambda c: (0, jnp.maximum(c * (n // TAIL) - 1, 0), 0))
    return row, tail


def _by_seq(a, bsz):
    return a.reshape(bsz, a.shape[0] // bsz, a.shape[1])


def _mlstm_call_parts(qk, v, o, small, conv_w, conv_b, gbias, n, bsz):
    row, tail = _seq_specs(n, bsz)
    qk, v, o, small = (_by_seq(a, bsz) for a in (qk, v, o, small))
    in_specs = [row(2 * ML_WIDTH), tail(2 * ML_WIDTH), row(ML_WIDTH), row(ML_WIDTH), row(LANES),
                _const_spec((CONV_K, 2 * ML_WIDTH)), _const_spec((1, 2 * ML_WIDTH)), _const_spec((1, LANES))]
    args = (qk, qk, v, o, small, conv_w, conv_b.reshape(1, -1), gbias)
    scratch = [pltpu.VMEM((bsz, ML_HEADS, ML_HEAD_DIM, 2 * ML_HEAD_DIM), F32),
               pltpu.VMEM((bsz, ML_HEADS, 1, LANES), F32)]
    return in_specs, args, 5, row(ML_WIDTH), jax.ShapeDtypeStruct(qk.shape[:2] + (ML_WIDTH,), BF16), scratch


def _ssd_kernel(z_ref, xbc_ref, tail_ref, small_ref, cw_ref, cb_ref, gbias_ref, alog_ref, dskip_ref, normw_ref, expand_ref,
                y_ref, state_scr):
    first = pl.program_id(0) == 0
    n = z_ref.shape[0]
    gw = M2_GROUP_WIDTH
    hd = M2_HEAD_DIM

    @pl.when(first)
    def _():
        state_scr[...] = jnp.zeros_like(state_scr)

    state_in = [state_scr[g] for g in range(M2_GROUPS)]
    xbc = _silu(_causal_conv(xbc_ref, tail_ref, cw_ref, cb_ref, first))
    xs = xbc[:, :M2_WIDTH]
    lane = lax.broadcasted_iota(jnp.int32, (1, LANES), 1)
    dt_lanes = (lane >= SLOT_DT) & (lane < SLOT_DT + M2_HEADS)
    dt = jnp.where(dt_lanes, jax.nn.softplus(small_ref[...] + gbias_ref[...]), 0.0)
    a_row = jnp.where(dt_lanes, -jnp.exp(alog_ref[...]), 0.0)
    tri = _lower_tri(n)
    cum = _dot3_left(tri.astype(BF16), dt * a_row)
    cum_t = cum.T
    cum_end = cum[n - 1:n, :]
    expand = expand_ref[...]
    dt_e = _dot(dt.astype(BF16), expand)
    decay_in = _dot(jnp.exp(cum).astype(BF16), expand)
    decay_out = _dot((dt * jnp.exp(cum_end - cum)).astype(BF16), expand)
    decay_chunk = _dot3_right(jnp.exp(cum[n - SUBLANES:n, :]), expand)[SUBLANES - 1:SUBLANES, :]
    xdt_b = (xs * dt_e).astype(BF16)
    x_in = (xs * decay_out).astype(BF16)
    left_half = lax.broadcasted_iota(jnp.int32, (n, 2 * hd), 1) < hd

    y_out, state_out = [], []
    for g in range(M2_GROUPS):
        b_g = xbc[:, M2_WIDTH + g * M2_STATE:M2_WIDTH + (g + 1) * M2_STATE]
        c_g = xbc[:, M2_WIDTH + (M2_GROUPS + g) * M2_STATE:M2_WIDTH + (M2_GROUPS + g + 1) * M2_STATE].astype(BF16)
        b_t = b_g.T.astype(BF16)
        cb = _dot(c_g, b_t)
        y_diag = []
        for pair in range(M2_HPG // 2):
            head = g * M2_HPG + 2 * pair
            ws = []
            for hh in (head, head + 1):
                cum_col = cum[:, SLOT_DT + hh:SLOT_DT + hh + 1]
                cum_row = cum_t[SLOT_DT + hh:SLOT_DT + hh + 1, :]
                ws.append((cb * jnp.exp(jnp.where(tri, cum_col - cum_row, -jnp.inf))).astype(BF16))
            xp = xdt_b[:, head * hd:(head + 2) * hd]
            zero = jnp.zeros_like(xp)
            rhs = jnp.concatenate([jnp.where(left_half, xp, zero), jnp.where(left_half, zero, xp)], axis=0)
            y_diag.append(_dot(jnp.concatenate(ws, axis=1), rhs))
        cols = slice(g * gw, (g + 1) * gw)
        st = state_in[g]
        y_off = _dot(c_g, st.astype(BF16)) * decay_in[:, cols]
        y = jnp.concatenate(y_diag, axis=1) + y_off + dskip_ref[:, cols] * xs[:, cols]
        y = y * _silu(z_ref[:, cols].astype(F32))
        y = y * lax.rsqrt(jnp.mean(y * y, axis=-1, keepdims=True) + EPS)
        y_out.append((y * normw_ref[:, cols]).astype(y_ref.dtype))
        state_out.append(st * decay_chunk[:, cols] + _dot(b_t, x_in[:, cols]))

    y_ref[...] = jnp.concatenate(y_out, axis=1)
    for g in range(M2_GROUPS):
        state_scr[g] = state_out[g]


def _ssd_call_parts(z, xbc, small, conv_w, conv_b, gbias, alog_row, dskip_row, norm_w, n, bsz):
    row, tail = _seq_specs(n, bsz)
    z, xbc, small = (_by_seq(a, bsz) for a in (z, xbc, small))
    expand = np.zeros((LANES, M2_WIDTH), np.float32)
    for h in range(M2_HEADS):
        expand[SLOT_DT + h, h * M2_HEAD_DIM:(h + 1) * M2_HEAD_DIM] = 1.0
    in_specs = [row(M2_WIDTH), row(M2_CONV_DIM), tail(M2_CONV_DIM), row(LANES),
                _const_spec((CONV_K, M2_CONV_DIM)), _const_spec((1, M2_CONV_DIM)), _const_spec((1, LANES)), _const_spec((1, LANES)),
                _const_spec((1, M2_WIDTH)), _const_spec((1, M2_WIDTH)), _const_spec((LANES, M2_WIDTH))]
    args = (z, xbc, xbc, small, conv_w, conv_b.reshape(1, -1), gbias, alog_row, dskip_row, norm_w.reshape(1, -1),
            jnp.asarray(expand, BF16))
    scratch = [pltpu.VMEM((bsz, M2_GROUPS, M2_STATE, M2_GROUP_WIDTH), F32)]
    return in_specs, args, 4, row(M2_WIDTH), jax.ShapeDtypeStruct(z.shape[:2] + (M2_WIDTH,), BF16), scratch


def _mixers(ml_inputs, ssd_inputs, bsz):
    t = ml_inputs[0].shape[0]
    seq = t // bsz
    n = min(SEQ_CHUNK, seq)
    ml_specs, ml_args, ml_nseq, ml_out_spec, ml_out_shape, ml_scratch = _mlstm_call_parts(*ml_inputs, n, bsz)
    ssd_specs, ssd_args, ssd_nseq, ssd_out_spec, ssd_out_shape, ssd_scratch = _ssd_call_parts(*ssd_inputs, n, bsz)
    n_ml, n_ssd = len(ml_args), len(ssd_args)

    def body(*refs):
        ins, outs, scr = refs[:n_ml + n_ssd], refs[n_ml + n_ssd:n_ml + n_ssd + 2], refs[n_ml + n_ssd + 2:]
        for b in range(bsz):
            per_seq = lambda rs, k: [r.at[b] if i < k else r for i, r in enumerate(rs)]
            _mlstm_kernel(*per_seq(ins[:n_ml], ml_nseq), outs[0].at[b], *[s.at[b] for s in scr[:len(ml_scratch)]])
            _ssd_kernel(*per_seq(ins[n_ml:], ssd_nseq), outs[1].at[b], *[s.at[b] for s in scr[len(ml_scratch):]])

    yb, yc = pl.pallas_call(
        body,
        grid=(seq // n,),
        in_specs=ml_specs + ssd_specs,
        out_specs=[ml_out_spec, ssd_out_spec],
        out_shape=[ml_out_shape, ssd_out_shape],
        scratch_shapes=ml_scratch + ssd_scratch,
        compiler_params=_cparams("arbitrary"),
        name="mixers",
    )(*ml_args, *ssd_args)
    return yb.reshape(t, ML_WIDTH), yc.reshape(t, M2_WIDTH)


def _merge_kernel(x_ref, yg_ref, yb_ref, yc_ref, g_ref, wglu_ref, wa_ref, wb_ref, wc_ref, wout_ref, nmp_ref, out_ref, ys_scr):
    d = D_MODEL
    gate = lambda i: _sigmoid(g_ref[:, i * d:(i + 1) * d].astype(F32))
    merged_bc = gate(1) * _dot(yb_ref[...], wb_ref[...]) + gate(2) * _dot(yc_ref[...], wc_ref[...])
    nchunks = yg_ref.shape[1]
    for jb in range(S5_GROUPS // GROUPS_PER_VREG):
        for half in range(S5_CHUNK // GROUPS_PER_VREG):
            v = [yg_ref[jb * GROUPS_PER_VREG + g, :, half * LANES:(half + 1) * LANES].astype(F32) for g in range(GROUPS_PER_VREG)]
            v = _block_transpose(v)
            for t in range(GROUPS_PER_VREG):
                ys_scr[jb, pl.ds(half * GROUPS_PER_VREG + t, nchunks, stride=S5_CHUNK), :] = v[t]
    ys5 = jnp.concatenate([ys_scr[jb] for jb in range(S5_WIDTH // LANES)], axis=1)
    ya = ys5 * _sigmoid(_dot(ys5.astype(BF16), wglu_ref[...]))
    merged = gate(0) * _dot(ya.astype(BF16), wa_ref[...]) + merged_bc
    mix = _dot(merged.astype(BF16), wout_ref[...])
    out_ref[...] = x_ref[...] + _rms(mix, nmp_ref[...])


def _merge(x2, yg, yb, yc, gates, weights, norm_post, layer):
    t = x2.shape[0]
    tm = min(TOKEN_BLOCK, t)
    row = lambda w: pl.BlockSpec((tm, w), lambda i: (i, 0))
    return pl.pallas_call(
        _merge_kernel,
        grid=(t // tm,),
        in_specs=[row(D_MODEL), pl.BlockSpec((S5_GROUPS, tm // S5_CHUNK, S5_K), lambda i: (0, i, 0)),
                  row(ML_WIDTH), row(M2_WIDTH), row(N_BRANCH * D_MODEL)]
        + [_layer_spec(w, layer) for w in weights] + [_const_spec((1, D_MODEL))],
        out_specs=row(D_MODEL),
        out_shape=jax.ShapeDtypeStruct((t, D_MODEL), F32),
        scratch_shapes=[pltpu.VMEM((S5_WIDTH // LANES, tm, LANES), F32)],
        compiler_params=_cparams("parallel"),
        name="merge",
    )(x2, yg, yb, yc, gates, *weights, norm_post.reshape(1, D_MODEL))


def _ffn_kernel(x_ref, npre_ref, npost_ref, wg_ref, wu_ref, wd_ref, out_ref):
    x = x_ref[...]
    hb = _rms(x, npre_ref[...]).astype(BF16)
    act = _silu(_dot(hb, wg_ref[...])) * _dot(hb, wu_ref[...])
    ffn = _dot(act.astype(BF16), wd_ref[...])
    out_ref[...] = x + _rms(ffn, npost_ref[...])


def _ffn(x2, weights, norm_pre, norm_post, layer):
    t = x2.shape[0]
    tm = min(TOKEN_BLOCK, t)
    row = pl.BlockSpec((tm, D_MODEL), lambda i: (i, 0))
    return pl.pallas_call(
        _ffn_kernel,
        grid=(t // tm,),
        in_specs=[row, _const_spec((1, D_MODEL)), _const_spec((1, D_MODEL))] + [_layer_spec(w, layer) for w in weights],
        out_specs=row,
        out_shape=jax.ShapeDtypeStruct((t, D_MODEL), F32),
        compiler_params=_cparams("parallel"),
        name="ffn",
    )(x2, norm_pre.reshape(1, D_MODEL), norm_post.reshape(1, D_MODEL), *weights)


def _gate_bias_row(bias_i, bias_f, dt_bias):
    pad = jnp.zeros((LANES - 2 * ML_HEADS - M2_HEADS,), F32)
    return jnp.concatenate([bias_i, bias_f, dt_bias, pad]).reshape(1, LANES)


def _head_lane_row(v):
    pad_l = jnp.zeros((SLOT_DT,), F32)
    pad_r = jnp.zeros((LANES - SLOT_DT - M2_HEADS,), F32)
    return jnp.concatenate([pad_l, v, pad_r]).reshape(1, LANES)


def _layer(x2, bsz, p, big, layer):
    ug, qk, v, o, z, xbc, gates, small = _in_proj(x2, p["norm_mix_pre"], big["w_in"], layer)
    mats = _s5_prep(p["s5_a_re"], p["s5_a_im"], p["s5_b_re"], p["s5_b_im"], p["s5_c_re"], p["s5_c_im"], p["s5_d"], p["s5_log_dt"])
    yg = _s5_mixer(ug, bsz, mats)
    gbias = _gate_bias_row(p["ml_bias_i"], p["ml_bias_f"], p["m2_dt_bias"])
    yb, yc = _mixers(
        (qk, v, o, small, p["ml_conv_w"], p["ml_conv_b"], gbias),
        (z, xbc, small, p["m2_conv_w"], p["m2_conv_b"], gbias, _head_lane_row(p["m2_a_log"]),
         jnp.repeat(p["m2_d"], M2_HEAD_DIM).reshape(1, M2_WIDTH), p["m2_norm_w"]),
        bsz)
    x1 = _merge(x2, yg, yb, yc, gates, big["merge"], p["norm_mix_post"], layer)
    return _ffn(x1, big["ffn"], p["norm_ffn_pre"], p["norm_ffn_post"], layer)


_PARAM_NAMES = ("norm_mix_pre", "norm_mix_post", "w_in", "s5_a_re", "s5_a_im", "s5_b_re", "s5_b_im", "s5_c_re", "s5_c_im", "s5_d",
                "s5_log_dt", "s5_w_glu", "ml_conv_w", "ml_conv_b", "ml_bias_i", "ml_bias_f", "m2_conv_w", "m2_conv_b", "m2_dt_bias",
                "m2_a_log", "m2_d", "m2_norm_w", "w_br_a", "w_br_b", "w_br_c", "w_out", "norm_ffn_pre", "norm_ffn_post",
                "w_ffn_gate", "w_ffn_up", "w_ffn_down")


def kernel(x, norm_mix_pre, norm_mix_post, w_in, s5_a_re, s5_a_im, s5_b_re, s5_b_im, s5_c_re, s5_c_im, s5_d, s5_log_dt, s5_w_glu, ml_conv_w, ml_conv_b, ml_bias_i, ml_bias_f, m2_conv_w, m2_conv_b, m2_dt_bias, m2_a_log, m2_d, m2_norm_w, w_br_a, w_br_b, w_br_c, w_out, norm_ffn_pre, norm_ffn_post, w_ffn_gate, w_ffn_up, w_ffn_down):
    stacked = (norm_mix_pre, norm_mix_post, w_in, s5_a_re, s5_a_im, s5_b_re, s5_b_im, s5_c_re, s5_c_im, s5_d, s5_log_dt, s5_w_glu,
               ml_conv_w, ml_conv_b, ml_bias_i, ml_bias_f, m2_conv_w, m2_conv_b, m2_dt_bias, m2_a_log, m2_d, m2_norm_w,
               w_br_a, w_br_b, w_br_c, w_out, norm_ffn_pre, norm_ffn_post, w_ffn_gate, w_ffn_up, w_ffn_down)
    bsz, seq, d = x.shape
    x2 = x.reshape(bsz * seq, d)
    big = {
        "w_in": _split_w_in(w_in),
        "merge": [w.astype(BF16) for w in (s5_w_glu, w_br_a, w_br_b, w_br_c, w_out)],
        "ffn": [w.astype(BF16) for w in (w_ffn_gate, w_ffn_up, w_ffn_down)],
    }
    for layer in range(norm_mix_pre.shape[0]):
        x2 = _layer(x2, bsz, {name: arr[layer] for name, arr in zip(_PARAM_NAMES, stacked)}, big, layer)
    return x2.reshape(bsz, seq, d)
```

```python
import jax
import jax.numpy as jnp
import numpy as np
from jax import lax
from jax.experimental import pallas as pl
from jax.experimental.pallas import tpu as pltpu

F32 = jnp.float32
BF16 = jnp.bfloat16

D_MODEL = 1024
EPS = 1e-6
CONV_K = 4

S5_WIDTH = 512
S5_GROUP = 16
S5_GROUPS = S5_WIDTH // S5_GROUP
S5_STATE = 64
S5_CHUNK = 16
S5_K = S5_CHUNK * S5_GROUP
S5_CHUNK_BLOCK = 256
S5_PREP_GROUPS = 4

ML_WIDTH = 512
ML_HEADS = 4
ML_HEAD_DIM = ML_WIDTH // ML_HEADS

M2_WIDTH = 1024
M2_HEAD_DIM = 64
M2_HEADS = M2_WIDTH // M2_HEAD_DIM
M2_GROUPS = 2
M2_HPG = M2_HEADS // M2_GROUPS
M2_STATE = 128
M2_CONV_DIM = M2_WIDTH + 2 * M2_GROUPS * M2_STATE
M2_GROUP_WIDTH = M2_WIDTH // M2_GROUPS

N_BRANCH = 3
FFN_HIDDEN = -(-(8 * D_MODEL) // (3 * 256)) * 256

LANES = 128
SUBLANES = 8
TAIL = SUBLANES
GROUPS_PER_VREG = LANES // S5_GROUP

SLOT_I = 0
SLOT_F = ML_HEADS
SLOT_DT = 2 * ML_HEADS

SEQ_CHUNK = 256
TOKEN_BLOCK = 512

VMEM_LIMIT = 56 * 1024 * 1024


def _cparams(*sem):
    return pltpu.CompilerParams(dimension_semantics=sem, vmem_limit_bytes=VMEM_LIMIT)


def _const_spec(shape):
    nd = len(shape)
    return pl.BlockSpec(shape, lambda *_: (0,) * nd, pipeline_mode=pl.Buffered(1))


def _layer_spec(stacked, layer):
    shape = stacked.shape[1:]
    zeros = (0,) * len(shape)
    return pl.BlockSpec((None,) + shape, lambda *_: (layer,) + zeros, pipeline_mode=pl.Buffered(1))


def _rms(x, w):
    return x * lax.rsqrt(jnp.mean(x * x, axis=-1, keepdims=True) + EPS) * w


def _split3(x):
    hi = x.astype(BF16)
    r1 = x - hi.astype(F32)
    mid = r1.astype(BF16)
    lo = (r1 - mid.astype(F32)).astype(BF16)
    return hi, mid, lo


def _dot(a, b):
    return jnp.dot(a, b, preferred_element_type=F32)


def _dot_nt(a, b):
    return lax.dot_general(a, b, (((1,), (1,)), ((), ())), preferred_element_type=F32)


def _dot3_left(m_bf16, x):
    hi, mid, lo = _split3(x)
    return _dot(m_bf16, hi) + _dot(m_bf16, mid) + _dot(m_bf16, lo)


def _dot3_right(x, m_bf16):
    hi, mid, lo = _split3(x)
    return _dot(hi, m_bf16) + _dot(mid, m_bf16) + _dot(lo, m_bf16)


def _lower_tri(n):
    r = lax.broadcasted_iota(jnp.int32, (n, n), 0)
    c = lax.broadcasted_iota(jnp.int32, (n, n), 1)
    return r >= c


def _block_transpose(v):
    lane_blk = lax.broadcasted_iota(jnp.int32, v[0].shape, 1) // S5_GROUP
    b = GROUPS_PER_VREG // 2
    while b:
        bit_set = (lane_blk & b) != 0
        out = list(v)
        for r in range(GROUPS_PER_VREG):
            if r & b == 0:
                rp = r | b
                out[r] = jnp.where(bit_set, pltpu.roll(v[rp], b * S5_GROUP, 1), v[r])
                out[rp] = jnp.where(bit_set, v[rp], pltpu.roll(v[r], LANES - b * S5_GROUP, 1))
        v = out
        b //= 2
    return v


IN_SEGMENTS = (
    ("qk", 2 * ML_WIDTH, BF16),
    ("v", ML_WIDTH, BF16),
    ("o", ML_WIDTH, BF16),
    ("z", M2_WIDTH, BF16),
    ("xbc", M2_CONV_DIM, BF16),
    ("gates", N_BRANCH * D_MODEL, BF16),
    ("small", LANES, F32),
)


W_A_WIDTH = S5_WIDTH + 4 * ML_WIDTH


_W_IN_SPLITS = (S5_WIDTH, ML_WIDTH, ML_WIDTH, ML_WIDTH, ML_WIDTH, ML_HEADS, ML_HEADS, M2_WIDTH, M2_CONV_DIM, M2_HEADS, N_BRANCH * D_MODEL)
_W_IN_OFFS = tuple(int(o) for o in np.cumsum((0,) + _W_IN_SPLITS))
W_IN_ROWS = 128


def _split_w_in_kernel(w_ref, wa_ref, wz_ref, wxbc_ref, wg_ref, wsmall_ref):
    o = _W_IN_OFFS
    col = lambda n0, n1: w_ref[0, :, o[n0]:o[n1]].astype(BF16)
    wa_ref[0] = col(0, 5)
    wz_ref[0] = col(7, 8)
    wxbc_ref[0] = col(8, 9)
    wg_ref[0] = col(10, 11)
    pad = jnp.zeros((w_ref.shape[1], LANES - 2 * ML_HEADS - M2_HEADS), BF16)
    wsmall_ref[0] = jnp.concatenate([col(5, 7), col(9, 10), pad], axis=-1)


def _split_w_in(w_in):
    layers, rows, width = w_in.shape
    widths = (W_A_WIDTH, M2_WIDTH, M2_CONV_DIM, N_BRANCH * D_MODEL, LANES)
    blk = lambda w: pl.BlockSpec((1, W_IN_ROWS, w), lambda l, r: (l, r, 0))
    return pl.pallas_call(
        _split_w_in_kernel,
        grid=(layers, rows // W_IN_ROWS),
        in_specs=[blk(width)],
        out_specs=[blk(w) for w in widths],
        out_shape=[jax.ShapeDtypeStruct((layers, rows, w), BF16) for w in widths],
        compiler_params=_cparams("parallel", "parallel"),
        name="split_w_in",
    )(w_in)


def _sigmoid(x):
    return 0.5 * jnp.tanh(0.5 * x) + 0.5


def _silu(x):
    h = 0.5 * x
    return h + h * jnp.tanh(h)


def _causal_conv(x_ref, tail_ref, w_ref, b_ref, first):
    n = x_ref.shape[0]
    x = x_ref[...].astype(F32)
    row = lax.broadcasted_iota(jnp.int32, (TAIL, 1), 0)
    shifts = [CONV_K - 1 - j for j in range(CONV_K - 1)]
    prev = jnp.where(first, 0.0, tail_ref[...].astype(F32))
    prev_rot = [pltpu.roll(prev, s, 0) for s in shifts]
    tiles = []
    for i in range(n // TAIL):
        cur = x[i * TAIL:(i + 1) * TAIL, :]
        cur_rot = [pltpu.roll(cur, s, 0) for s in shifts]
        acc = b_ref[...] + w_ref[CONV_K - 1:CONV_K, :] * cur
        for j, s in enumerate(shifts):
            acc = acc + w_ref[j:j + 1, :] * jnp.where(row < s, prev_rot[j], cur_rot[j])
        tiles.append(acc)
        prev_rot = cur_rot
    return jnp.concatenate(tiles, axis=0)


def _in_proj_kernel(x_ref, nw_ref, wa_ref, wz_ref, wxbc_ref, wg_ref, wsmall_ref,
                    ug_ref, qk_ref, v_ref, o_ref, z_ref, xbc_ref, g_ref, small_ref, u_scr):
    hb = _rms(x_ref[...], nw_ref[...]).astype(BF16)
    u = _dot(hb, wa_ref[:, 0:S5_WIDTH])
    off = S5_WIDTH
    for ref in (qk_ref, v_ref, o_ref):
        width = ref.shape[1]
        ref[...] = _dot(hb, wa_ref[:, off:off + width]).astype(ref.dtype)
        off += width
    for ref, w_ref in ((z_ref, wz_ref), (xbc_ref, wxbc_ref), (g_ref, wg_ref), (small_ref, wsmall_ref)):
        ref[...] = _dot(hb, w_ref[...]).astype(ref.dtype)

    nchunks = u_scr.shape[1] // S5_CHUNK
    for jb in range(S5_GROUPS // GROUPS_PER_VREG):
        u_scr[jb] = u[:, jb * LANES:(jb + 1) * LANES]
        for half in range(S5_CHUNK // GROUPS_PER_VREG):
            v = [u_scr[jb, pl.ds(half * GROUPS_PER_VREG + t, nchunks, stride=S5_CHUNK), :] for t in range(GROUPS_PER_VREG)]
            v = _block_transpose(v)
            for g in range(GROUPS_PER_VREG):
                ug_ref[jb * GROUPS_PER_VREG + g, :, half * LANES:(half + 1) * LANES] = v[g].astype(ug_ref.dtype)


def _in_proj(x2, norm_w, w_parts, layer):
    t = x2.shape[0]
    tm = min(TOKEN_BLOCK, t)
    nchunks = tm // S5_CHUNK
    return pl.pallas_call(
        _in_proj_kernel,
        grid=(t // tm,),
        in_specs=[pl.BlockSpec((tm, D_MODEL), lambda i: (i, 0)), _const_spec((1, D_MODEL))]
        + [_layer_spec(w, layer) for w in w_parts],
        out_specs=[pl.BlockSpec((S5_GROUPS, nchunks, S5_K), lambda i: (0, i, 0))]
        + [pl.BlockSpec((tm, w), lambda i: (i, 0)) for _, w, _ in IN_SEGMENTS],
        out_shape=[jax.ShapeDtypeStruct((S5_GROUPS, t // S5_CHUNK, S5_K), BF16)]
        + [jax.ShapeDtypeStruct((t, w), dt) for _, w, dt in IN_SEGMENTS],
        scratch_shapes=[pltpu.VMEM((S5_WIDTH // LANES, tm, LANES), F32)],
        compiler_params=_cparams("parallel"),
        name="in_proj",
    )(x2, norm_w.reshape(1, D_MODEL), *w_parts)


def _s5_prep_kernel(*refs):
    for gi in range(refs[0].shape[0]):
        _s5_prep_group(*[r.at[gi:gi + 1] for r in refs])


def _s5_prep_group(lr_ref, li_ref, logdt_ref, bt_ref, bti_ref, btr_t_ref, bti_t_ref, cr_ref, ci_ref, d_ref,
                   tt_ref, w_ref, mt_ref, pq_ref):
    p64 = S5_STATE
    lr, li = lr_ref[0], li_ref[0]
    dt = jnp.exp(logdt_ref[0])
    mag = jnp.exp(lr * dt)
    lb_re = mag * jnp.cos(li * dt)
    lb_im = mag * jnp.sin(li * dt)
    den = lr * lr + li * li
    f_re = ((lb_re - 1.0) * lr + lb_im * li) / den
    f_im = (lb_im * lr - (lb_re - 1.0) * li) / den

    pw = [(jnp.ones_like(lb_re), jnp.zeros_like(lb_im))]
    for _ in range(S5_CHUNK):
        pr, pi = pw[-1]
        pw.append((pr * lb_re - pi * lb_im, pr * lb_im + pi * lb_re))

    cr, ci = cr_ref[0], ci_ref[0]
    btr_t, bti_t = btr_t_ref[0], bti_t_ref[0]
    g_re, g_im, m_re, m_im, w_re, w_im = [], [], [], [], [], []
    for k in range(S5_CHUNK):
        pr, pi = pw[k]
        afr = pr * f_re - pi * f_im
        afi = pr * f_im + pi * f_re
        g_re.append(afr * cr - afi * ci)
        g_im.append(afr * ci + afi * cr)
        qr, qi = pw[k + 1]
        m_re.append(qr * cr - qi * ci)
        m_im.append(qr * ci + qi * cr)
        sr, si = pw[S5_CHUNK - 1 - k]
        wfr = sr * f_re - si * f_im
        wfi = sr * f_im + si * f_re
        w_re.append(wfr * btr_t - wfi * bti_t)
        w_im.append(wfr * bti_t + wfi * btr_t)
    cat = lambda blocks: jnp.concatenate(blocks, axis=0)
    w_ref[0, :, 0:p64] = cat(w_re).astype(w_ref.dtype)
    w_ref[0, :, p64:2 * p64] = cat(w_im).astype(w_ref.dtype)
    mt_ref[0, :, 0:p64] = cat(m_re).astype(mt_ref.dtype)
    mt_ref[0, :, p64:2 * p64] = (-cat(m_im)).astype(mt_ref.dtype)

    hi = lax.Precision.HIGHEST
    kk = (jnp.dot(cat(g_re), bt_ref[0], precision=hi, preferred_element_type=F32)
          - jnp.dot(cat(g_im), bti_ref[0], precision=hi, preferred_element_type=F32))
    rows = lax.broadcasted_iota(jnp.int32, (S5_K, S5_K), 0)
    cols = lax.broadcasted_iota(jnp.int32, (S5_K, S5_K), 1)
    col_blk = cols // S5_GROUP
    tt = jnp.where(rows == cols, d_ref[0], 0.0)
    for s in range(S5_CHUNK):
        if s == 0:
            shifted = kk
        else:
            shifted = jnp.concatenate([jnp.zeros((s * S5_GROUP, S5_K), F32), kk[:S5_K - s * S5_GROUP, :]], axis=0)
        tt = tt + jnp.where(col_blk == s, shifted, 0.0)
    tt_ref[0] = tt.astype(tt_ref.dtype)

    cr16, ci16 = pw[S5_CHUNK]
    pq_ref[0, 0:1, :] = jnp.concatenate([cr16, cr16], axis=1)
    pq_ref[0, 1:2, :] = jnp.concatenate([-ci16, ci16], axis=1)


def _s5_prep(a_re, a_im, b_re, b_im, c_re, c_im, d_skip, log_dt):
    g, p = S5_GROUPS, S5_STATE
    tile_lanes = lambda b: jnp.tile(b, (1, 1, S5_CHUNK))
    swap = lambda b: jnp.transpose(b, (0, 2, 1))
    args = (
        a_re.reshape(g, 1, p), a_im.reshape(g, 1, p), log_dt.reshape(g, 1, 1),
        tile_lanes(b_re), tile_lanes(b_im), swap(b_re), swap(b_im), c_re, c_im,
        jnp.tile(d_skip, (1, S5_CHUNK)).reshape(g, 1, S5_K),
    )
    spec = lambda a: pl.BlockSpec((S5_PREP_GROUPS,) + a.shape[1:], lambda i: (i, 0, 0))
    out_shapes = (
        jax.ShapeDtypeStruct((g, S5_K, S5_K), BF16),
        jax.ShapeDtypeStruct((g, S5_K, 2 * p), BF16),
        jax.ShapeDtypeStruct((g, S5_K, 2 * p), BF16),
        jax.ShapeDtypeStruct((g, 2, 2 * p), F32),
    )
    return pl.pallas_call(
        _s5_prep_kernel,
        grid=(g // S5_PREP_GROUPS,),
        in_specs=[spec(a) for a in args],
        out_specs=[spec(o) for o in out_shapes],
        out_shape=out_shapes,
        compiler_params=_cparams("parallel"),
        name="s5_prep",
    )(*args)


def _s5_kernel(ug_ref, tt_ref, w_ref, mt_ref, pq_ref, yg_ref, l_scr, s_scr, state_scr):
    ng = S5_GROUPS
    nchunks = ug_ref.shape[1]

    @pl.when(pl.program_id(1) == 0)
    def _():
        state_scr[...] = jnp.zeros_like(state_scr)

    def local(g, carry):
        l_scr[pl.ds(g, nchunks, stride=ng), :] = _dot(ug_ref[g], w_ref[g])
        return carry

    lax.fori_loop(0, ng, local, 0, unroll=8)

    p_mul = pq_ref[0]
    q_mul = pq_ref[1]

    def step(j, carry):
        x, xs = carry
        row = pl.multiple_of(j * ng, ng)
        s_scr[pl.ds(row, ng), :] = x
        l = l_scr[pl.ds(row, ng), :]
        ls = pltpu.roll(l, S5_STATE, 1)
        return p_mul * x + q_mul * xs + l, p_mul * xs - q_mul * x + ls

    x, xs = lax.fori_loop(0, nchunks, step, (state_scr[0], state_scr[1]), unroll=8)
    state_scr[0] = x
    state_scr[1] = xs

    def output(g, carry):
        u = ug_ref[g]
        s_in = s_scr[pl.ds(g, nchunks, stride=ng), :].astype(BF16)
        y = _dot_nt(u, tt_ref[g]) + _dot_nt(s_in, mt_ref[g])
        yg_ref[g] = jax.nn.gelu(y).astype(yg_ref.dtype)
        return carry

    lax.fori_loop(0, ng, output, 0, unroll=8)


def _s5_mixer(ug, bsz, mats):
    tt, w_mat, mt, pq = mats
    pq = jnp.transpose(pq, (1, 0, 2))
    g, nch, _ = ug.shape
    nch_b = nch // bsz
    cb = min(S5_CHUNK_BLOCK, nch_b)
    nblk = nch_b // cb
    blk = pl.BlockSpec((g, cb, S5_K), lambda b, j: (0, b * nblk + j, 0))
    return pl.pallas_call(
        _s5_kernel,
        grid=(bsz, nblk),
        in_specs=[blk, _const_spec(tt.shape), _const_spec(w_mat.shape), _const_spec(mt.shape), _const_spec(pq.shape)],
        out_specs=blk,
        out_shape=jax.ShapeDtypeStruct(ug.shape, BF16),
        scratch_shapes=[
            pltpu.VMEM((cb * g, 2 * S5_STATE), F32),
            pltpu.VMEM((cb * g, 2 * S5_STATE), F32),
            pltpu.VMEM((2, g, 2 * S5_STATE), F32),
        ],
        compiler_params=_cparams("arbitrary", "arbitrary"),
        name="s5",
    )(ug, tt, w_mat, mt, pq)


def _mlstm_kernel(qk_ref, tail_ref, v_ref, o_ref, small_ref, cw_ref, cb_ref, gbias_ref, y_ref, ct_scr, m_scr):
    first = pl.program_id(0) == 0
    n = qk_ref.shape[0]
    dh = ML_HEAD_DIM

    @pl.when(first)
    def _():
        ct_scr[...] = jnp.zeros_like(ct_scr)
        m_scr[...] = jnp.zeros_like(m_scr)

    ct_in = [ct_scr[h] for h in range(ML_HEADS)]
    m_in = [m_scr[h][:, 0:1] for h in range(ML_HEADS)]
    qk = _silu(_causal_conv(qk_ref, tail_ref, cw_ref, cb_ref, first))
    gate = small_ref[...] + gbias_ref[...]
    lf = jax.nn.log_sigmoid(gate)
    tri = _lower_tri(n)
    b_all = _dot3_left(tri.astype(BF16), lf)
    gate_t = gate.T
    b_t = b_all.T
    ones_col = (lax.broadcasted_iota(jnp.int32, (n, dh), 1) == 0).astype(BF16)

    outs, ct_out, m_out = [], [], []
    for h in range(ML_HEADS):
        q = qk[:, h * dh:(h + 1) * dh].astype(BF16)
        k = qk[:, ML_WIDTH + h * dh:ML_WIDTH + (h + 1) * dh] * (dh ** -0.5)
        k_t = k.T.astype(BF16)
        v_ext = jnp.concatenate([v_ref[:, h * dh:(h + 1) * dh], ones_col], axis=1)
        b_col = b_all[:, SLOT_F + h:SLOT_F + h + 1]
        ig_col = gate[:, SLOT_I + h:SLOT_I + h + 1]
        b_row = b_t[SLOT_F + h:SLOT_F + h + 1, :]
        ig_row = gate_t[SLOT_I + h:SLOT_I + h + 1, :]
        m_prev = m_in[h]
        ct_prev = ct_in[h]

        dmat = jnp.where(tri, b_col - b_row + ig_row, -jnp.inf)
        inter = b_col + m_prev
        m_t = jnp.maximum(jnp.max(dmat, axis=1, keepdims=True), inter)
        p = _dot(q, k_t) * jnp.exp(dmat - m_t)
        w_inter = jnp.exp(inter - m_t)
        num_den = _dot(p.astype(BF16), v_ext) + w_inter * _dot(q, ct_prev.astype(BF16))
        den = num_den[:, dh:dh + 1]
        denom = jnp.maximum(jnp.abs(den), jnp.exp(-m_t))
        out = num_den[:, :dh] / denom * _sigmoid(o_ref[:, h * dh:(h + 1) * dh].astype(F32))
        outs.append(out.astype(y_ref.dtype))

        b_end = b_col[n - 1:n, :]
        a_end = b_end - b_col + ig_col
        m_loc = jnp.max(a_end, axis=0, keepdims=True)
        w_end = jnp.exp(a_end - m_loc)
        m_new = jnp.maximum(b_end + m_prev, m_loc)
        s_prev = jnp.exp(b_end + m_prev - m_new)
        s_loc = jnp.exp(m_loc - m_new)
        upd = _dot(k_t, (w_end * v_ext.astype(F32)).astype(BF16))
        ct_out.append(s_prev * ct_prev + s_loc * upd)
        m_out.append(jnp.broadcast_to(m_new, m_scr.shape[1:]))

    y_ref[...] = jnp.concatenate(outs, axis=1)
    for h in range(ML_HEADS):
        ct_scr[h] = ct_out[h]
        m_scr[h] = m_out[h]


def _seq_specs(n, bsz):
    row = lambda w: pl.BlockSpec((bsz, n, w), lambda c: (0, c, 0))
    tail = lambda w: pl.BlockSpec((bsz, TAIL, w), lambda c: (0, jnp.maximum(c * (n // TAIL) - 1, 0), 0))
    return row, tail


def _by_seq(a, bsz):
    return a.reshape(bsz, a.shape[0] // bsz, a.shape[1])


def _mlstm_call_parts(qk, v, o, small, conv_w, conv_b, gbias, n, bsz):
    row, tail = _seq_specs(n, bsz)
    qk, v, o, small = (_by_seq(a, bsz) for a in (qk, v, o, small))
    in_specs = [row(2 * ML_WIDTH), tail(2 * ML_WIDTH), row(ML_WIDTH), row(ML_WIDTH), row(LANES),
                _const_spec((CONV_K, 2 * ML_WIDTH)), _const_spec((1, 2 * ML_WIDTH)), _const_spec((1, LANES))]
    args = (qk, qk, v, o, small, conv_w, conv_b.reshape(1, -1), gbias)
    scratch = [pltpu.VMEM((bsz, ML_HEADS, ML_HEAD_DIM, 2 * ML_HEAD_DIM), F32),
               pltpu.VMEM((bsz, ML_HEADS, 1, LANES), F32)]
    return in_specs, args, 5, row(ML_WIDTH), jax.ShapeDtypeStruct(qk.shape[:2] + (ML_WIDTH,), BF16), scratch


def _ssd_kernel(z_ref, xbc_ref, tail_ref, small_ref, cw_ref, cb_ref, gbias_ref, alog_ref, dskip_ref, normw_ref, expand_ref,
                y_ref, state_scr):
    first = pl.program_id(0) == 0
    n = z_ref.shape[0]
    gw = M2_GROUP_WIDTH
    hd = M2_HEAD_DIM

    @pl.when(first)
    def _():
        state_scr[...] = jnp.zeros_like(state_scr)

    state_in = [state_scr[g] for g in range(M2_GROUPS)]
    xbc = _silu(_causal_conv(xbc_ref, tail_ref, cw_ref, cb_ref, first))
    xs = xbc[:, :M2_WIDTH]
    lane = lax.broadcasted_iota(jnp.int32, (1, LANES), 1)
    dt_lanes = (lane >= SLOT_DT) & (lane < SLOT_DT + M2_HEADS)
    dt = jnp.where(dt_lanes, jax.nn.softplus(small_ref[...] + gbias_ref[...]), 0.0)
    a_row = jnp.where(dt_lanes, -jnp.exp(alog_ref[...]), 0.0)
    tri = _lower_tri(n)
    cum = _dot3_left(tri.astype(BF16), dt * a_row)
    cum_t = cum.T
    cum_end = cum[n - 1:n, :]
    expand = expand_ref[...]
    dt_e = _dot(dt.astype(BF16), expand)
    decay_in = _dot(jnp.exp(cum).astype(BF16), expand)
    decay_out = _dot((dt * jnp.exp(cum_end - cum)).astype(BF16), expand)
    decay_chunk = _dot3_right(jnp.exp(cum[n - SUBLANES:n, :]), expand)[SUBLANES - 1:SUBLANES, :]
    xdt_b = (xs * dt_e).astype(BF16)
    x_in = (xs * decay_out).astype(BF16)
    left_half = lax.broadcasted_iota(jnp.int32, (n, 2 * hd), 1) < hd

    y_out, state_out = [], []
    for g in range(M2_GROUPS):
        b_g = xbc[:, M2_WIDTH + g * M2_STATE:M2_WIDTH + (g + 1) * M2_STATE]
        c_g = xbc[:, M2_WIDTH + (M2_GROUPS + g) * M2_STATE:M2_WIDTH + (M2_GROUPS + g + 1) * M2_STATE].astype(BF16)
        b_t = b_g.T.astype(BF16)
        cb = _dot(c_g, b_t)
        y_diag = []
        for pair in range(M2_HPG // 2):
            head = g * M2_HPG + 2 * pair
            ws = []
            for hh in (head, head + 1):
                cum_col = cum[:, SLOT_DT + hh:SLOT_DT + hh + 1]
                cum_row = cum_t[SLOT_DT + hh:SLOT_DT + hh + 1, :]
                ws.append((cb * jnp.exp(jnp.where(tri, cum_col - cum_row, -jnp.inf))).astype(BF16))
            xp = xdt_b[:, head * hd:(head + 2) * hd]
            zero = jnp.zeros_like(xp)
            rhs = jnp.concatenate([jnp.where(left_half, xp, zero), jnp.where(left_half, zero, xp)], axis=0)
            y_diag.append(_dot(jnp.concatenate(ws, axis=1), rhs))
        cols = slice(g * gw, (g + 1) * gw)
        st = state_in[g]
        y_off = _dot(c_g, st.astype(BF16)) * decay_in[:, cols]
        y = jnp.concatenate(y_diag, axis=1) + y_off + dskip_ref[:, cols] * xs[:, cols]
        y = y * _silu(z_ref[:, cols].astype(F32))
        y = y * lax.rsqrt(jnp.mean(y * y, axis=-1, keepdims=True) + EPS)
        y_out.append((y * normw_ref[:, cols]).astype(y_ref.dtype))
        state_out.append(st * decay_chunk[:, cols] + _dot(b_t, x_in[:, cols]))

    y_ref[...] = jnp.concatenate(y_out, axis=1)
    for g in range(M2_GROUPS):
        state_scr[g] = state_out[g]


def _ssd_call_parts(z, xbc, small, conv_w, conv_b, gbias, alog_row, dskip_row, norm_w, n, bsz):
    row, tail = _seq_specs(n, bsz)
    z, xbc, small = (_by_seq(a, bsz) for a in (z, xbc, small))
    expand = np.zeros((LANES, M2_WIDTH), np.float32)
    for h in range(M2_HEADS):
        expand[SLOT_DT + h, h * M2_HEAD_DIM:(h + 1) * M2_HEAD_DIM] = 1.0
    in_specs = [row(M2_WIDTH), row(M2_CONV_DIM), tail(M2_CONV_DIM), row(LANES),
                _const_spec((CONV_K, M2_CONV_DIM)), _const_spec((1, M2_CONV_DIM)), _const_spec((1, LANES)), _const_spec((1, LANES)),
                _const_spec((1, M2_WIDTH)), _const_spec((1, M2_WIDTH)), _const_spec((LANES, M2_WIDTH))]
    args = (z, xbc, xbc, small, conv_w, conv_b.reshape(1, -1), gbias, alog_row, dskip_row, norm_w.reshape(1, -1),
            jnp.asarray(expand, BF16))
    scratch = [pltpu.VMEM((bsz, M2_GROUPS, M2_STATE, M2_GROUP_WIDTH), F32)]
    return in_specs, args, 4, row(M2_WIDTH), jax.ShapeDtypeStruct(z.shape[:2] + (M2_WIDTH,), BF16), scratch


def _mixers(ml_inputs, ssd_inputs, bsz):
    t = ml_inputs[0].shape[0]
    seq = t // bsz
    n = min(SEQ_CHUNK, seq)
    ml_specs, ml_args, ml_nseq, ml_out_spec, ml_out_shape, ml_scratch = _mlstm_call_parts(*ml_inputs, n, bsz)
    ssd_specs, ssd_args, ssd_nseq, ssd_out_spec, ssd_out_shape, ssd_scratch = _ssd_call_parts(*ssd_inputs, n, bsz)
    n_ml, n_ssd = len(ml_args), len(ssd_args)

    def body(*refs):
        ins, outs, scr = refs[:n_ml + n_ssd], refs[n_ml + n_ssd:n_ml + n_ssd + 2], refs[n_ml + n_ssd + 2:]
        for b in range(bsz):
            per_seq = lambda rs, k: [r.at[b] if i < k else r for i, r in enumerate(rs)]
            _mlstm_kernel(*per_seq(ins[:n_ml], ml_nseq), outs[0].at[b], *[s.at[b] for s in scr[:len(ml_scratch)]])
            _ssd_kernel(*per_seq(ins[n_ml:], ssd_nseq), outs[1].at[b], *[s.at[b] for s in scr[len(ml_scratch):]])

    yb, yc = pl.pallas_call(
        body,
        grid=(seq // n,),
        in_specs=ml_specs + ssd_specs,
        out_specs=[ml_out_spec, ssd_out_spec],
        out_shape=[ml_out_shape, ssd_out_shape],
        scratch_shapes=ml_scratch + ssd_scratch,
        compiler_params=_cparams("arbitrary"),
        name="mixers",
    )(*ml_args, *ssd_args)
    return yb.reshape(t, ML_WIDTH), yc.reshape(t, M2_WIDTH)


def _merge_kernel(x_ref, yg_ref, yb_ref, yc_ref, g_ref, wglu_ref, wa_ref, wb_ref, wc_ref, wout_ref, nmp_ref, out_ref, ys_scr):
    d = D_MODEL
    gate = lambda i: _sigmoid(g_ref[:, i * d:(i + 1) * d].astype(F32))
    merged_bc = gate(1) * _dot(yb_ref[...], wb_ref[...]) + gate(2) * _dot(yc_ref[...], wc_ref[...])
    nchunks = yg_ref.shape[1]
    for jb in range(S5_GROUPS // GROUPS_PER_VREG):
        for half in range(S5_CHUNK // GROUPS_PER_VREG):
            v = [yg_ref[jb * GROUPS_PER_VREG + g, :, half * LANES:(half + 1) * LANES].astype(F32) for g in range(GROUPS_PER_VREG)]
            v = _block_transpose(v)
            for t in range(GROUPS_PER_VREG):
                ys_scr[jb, pl.ds(half * GROUPS_PER_VREG + t, nchunks, stride=S5_CHUNK), :] = v[t]
    ys5 = jnp.concatenate([ys_scr[jb] for jb in range(S5_WIDTH // LANES)], axis=1)
    ya = ys5 * _sigmoid(_dot(ys5.astype(BF16), wglu_ref[...]))
    merged = gate(0) * _dot(ya.astype(BF16), wa_ref[...]) + merged_bc
    mix = _dot(merged.astype(BF16), wout_ref[...])
    out_ref[...] = x_ref[...] + _rms(mix, nmp_ref[...])


def _merge(x2, yg, yb, yc, gates, weights, norm_post, layer):
    t = x2.shape[0]
    tm = min(TOKEN_BLOCK, t)
    row = lambda w: pl.BlockSpec((tm, w), lambda i: (i, 0))
    return pl.pallas_call(
        _merge_kernel,
        grid=(t // tm,),
        in_specs=[row(D_MODEL), pl.BlockSpec((S5_GROUPS, tm // S5_CHUNK, S5_K), lambda i: (0, i, 0)),
                  row(ML_WIDTH), row(M2_WIDTH), row(N_BRANCH * D_MODEL)]
        + [_layer_spec(w, layer) for w in weights] + [_const_spec((1, D_MODEL))],
        out_specs=row(D_MODEL),
        out_shape=jax.ShapeDtypeStruct((t, D_MODEL), F32),
        scratch_shapes=[pltpu.VMEM((S5_WIDTH // LANES, tm, LANES), F32)],
        compiler_params=_cparams("parallel"),
        name="merge",
    )(x2, yg, yb, yc, gates, *weights, norm_post.reshape(1, D_MODEL))


def _ffn_kernel(x_ref, npre_ref, npost_ref, wg_ref, wu_ref, wd_ref, out_ref):
    x = x_ref[...]
    hb = _rms(x, npre_ref[...]).astype(BF16)
    act = _silu(_dot(hb, wg_ref[...])) * _dot(hb, wu_ref[...])
    ffn = _dot(act.astype(BF16), wd_ref[...])
    out_ref[...] = x + _rms(ffn, npost_ref[...])


def _ffn(x2, weights, norm_pre, norm_post, layer):
    t = x2.shape[0]
    tm = min(TOKEN_BLOCK, t)
    row = pl.BlockSpec((tm, D_MODEL), lambda i: (i, 0))
    return pl.pallas_call(
        _ffn_kernel,
        grid=(t // tm,),
        in_specs=[row, _const_spec((1, D_MODEL)), _const_spec((1, D_MODEL))] + [_layer_spec(w, layer) for w in weights],
        out_specs=row,
        out_shape=jax.ShapeDtypeStruct((t, D_MODEL), F32),
        compiler_params=_cparams("parallel"),
        name="ffn",
    )(x2, norm_pre.reshape(1, D_MODEL), norm_post.reshape(1, D_MODEL), *weights)


def _gate_bias_row(bias_i, bias_f, dt_bias):
    pad = jnp.zeros((LANES - 2 * ML_HEADS - M2_HEADS,), F32)
    return jnp.concatenate([bias_i, bias_f, dt_bias, pad]).reshape(1, LANES)


def _head_lane_row(v):
    pad_l = jnp.zeros((SLOT_DT,), F32)
    pad_r = jnp.zeros((LANES - SLOT_DT - M2_HEADS,), F32)
    return jnp.concatenate([pad_l, v, pad_r]).reshape(1, LANES)


def _layer(x2, bsz, p, big, layer):
    ug, qk, v, o, z, xbc, gates, small = _in_proj(x2, p["norm_mix_pre"], big["w_in"], layer)
    mats = _s5_prep(p["s5_a_re"], p["s5_a_im"], p["s5_b_re"], p["s5_b_im"], p["s5_c_re"], p["s5_c_im"], p["s5_d"], p["s5_log_dt"])
    yg = _s5_mixer(ug, bsz, mats)
    gbias = _gate_bias_row(p["ml_bias_i"], p["ml_bias_f"], p["m2_dt_bias"])
    yb, yc = _mixers(
        (qk, v, o, small, p["ml_conv_w"], p["ml_conv_b"], gbias),
        (z, xbc, small, p["m2_conv_w"], p["m2_conv_b"], gbias, _head_lane_row(p["m2_a_log"]),
         jnp.repeat(p["m2_d"], M2_HEAD_DIM).reshape(1, M2_WIDTH), p["m2_norm_w"]),
        bsz)
    x1 = _merge(x2, yg, yb, yc, gates, big["merge"], p["norm_mix_post"], layer)
    return _ffn(x1, big["ffn"], p["norm_ffn_pre"], p["norm_ffn_post"], layer)


_PARAM_NAMES = ("norm_mix_pre", "norm_mix_post", "w_in", "s5_a_re", "s5_a_im", "s5_b_re", "s5_b_im", "s5_c_re", "s5_c_im", "s5_d",
                "s5_log_dt", "s5_w_glu", "ml_conv_w", "ml_conv_b", "ml_bias_i", "ml_bias_f", "m2_conv_w", "m2_conv_b", "m2_dt_bias",
                "m2_a_log", "m2_d", "m2_norm_w", "w_br_a", "w_br_b", "w_br_c", "w_out", "norm_ffn_pre", "norm_ffn_post",
                "w_ffn_gate", "w_ffn_up", "w_ffn_down")


def kernel(x, norm_mix_pre, norm_mix_post, w_in, s5_a_re, s5_a_im, s5_b_re, s5_b_im, s5_c_re, s5_c_im, s5_d, s5_log_dt, s5_w_glu, ml_conv_w, ml_conv_b, ml_bias_i, ml_bias_f, m2_conv_w, m2_conv_b, m2_dt_bias, m2_a_log, m2_d, m2_norm_w, w_br_a, w_br_b, w_br_c, w_out, norm_ffn_pre, norm_ffn_post, w_ffn_gate, w_ffn_up, w_ffn_down):
    stacked = (norm_mix_pre, norm_mix_post, w_in, s5_a_re, s5_a_im, s5_b_re, s5_b_im, s5_c_re, s5_c_im, s5_d, s5_log_dt, s5_w_glu,
               ml_conv_w, ml_conv_b, ml_bias_i, ml_bias_f, m2_conv_w, m2_conv_b, m2_dt_bias, m2_a_log, m2_d, m2_norm_w,
               w_br_a, w_br_b, w_br_c, w_out, norm_ffn_pre, norm_ffn_post, w_ffn_gate, w_ffn_up, w_ffn_down)
    bsz, seq, d = x.shape
    x2 = x.reshape(bsz * seq, d)
    big = {
        "w_in": _split_w_in(w_in),
        "merge": [w.astype(BF16) for w in (s5_w_glu, w_br_a, w_br_b, w_br_c, w_out)],
        "ffn": [w.astype(BF16) for w in (w_ffn_gate, w_ffn_up, w_ffn_down)],
    }
    for layer in range(norm_mix_pre.shape[0]):
        x2 = _layer(x2, bsz, {name: arr[layer] for name, arr in zip(_PARAM_NAMES, stacked)}, big, layer)
    return x2.reshape(bsz, seq, d)
```

```python
import jax
import jax.numpy as jnp
import numpy as np
from jax import lax
from jax.experimental import pallas as pl
from jax.experimental.pallas import tpu as pltpu

F32 = jnp.float32
BF16 = jnp.bfloat16

D_MODEL = 1024
EPS = 1e-6
CONV_K = 4

S5_WIDTH = 512
S5_GROUP = 16
S5_GROUPS = S5_WIDTH // S5_GROUP
S5_STATE = 64
S5_CHUNK = 16
S5_K = S5_CHUNK * S5_GROUP
S5_CHUNK_BLOCK = 256
S5_PREP_GROUPS = 8

ML_WIDTH = 512
ML_HEADS = 4
ML_HEAD_DIM = ML_WIDTH // ML_HEADS

M2_WIDTH = 1024
M2_HEAD_DIM = 64
M2_HEADS = M2_WIDTH // M2_HEAD_DIM
M2_GROUPS = 2
M2_HPG = M2_HEADS // M2_GROUPS
M2_STATE = 128
M2_CONV_DIM = M2_WIDTH + 2 * M2_GROUPS * M2_STATE
M2_GROUP_WIDTH = M2_WIDTH // M2_GROUPS

N_BRANCH = 3
FFN_HIDDEN = -(-(8 * D_MODEL) // (3 * 256)) * 256

LANES = 128
SUBLANES = 8
TAIL = SUBLANES
GROUPS_PER_VREG = LANES // S5_GROUP

SLOT_I = 0
SLOT_F = ML_HEADS
SLOT_DT = 2 * ML_HEADS

SEQ_CHUNK = 256
TOKEN_BLOCK = 512
WIDE_TOKEN_BLOCK = 1024

VMEM_LIMIT = 56 * 1024 * 1024


def _cparams(*sem):
    return pltpu.CompilerParams(dimension_semantics=sem, vmem_limit_bytes=VMEM_LIMIT)


def _const_spec(shape):
    nd = len(shape)
    return pl.BlockSpec(shape, lambda *_: (0,) * nd, pipeline_mode=pl.Buffered(1))


def _layer_spec(stacked, layer):
    shape = stacked.shape[1:]
    zeros = (0,) * len(shape)
    return pl.BlockSpec((None,) + shape, lambda *_: (layer,) + zeros, pipeline_mode=pl.Buffered(1))


def _rms(x, w):
    return x * lax.rsqrt(jnp.mean(x * x, axis=-1, keepdims=True) + EPS) * w


def _split3(x):
    hi = x.astype(BF16)
    r1 = x - hi.astype(F32)
    mid = r1.astype(BF16)
    lo = (r1 - mid.astype(F32)).astype(BF16)
    return hi, mid, lo


def _dot(a, b):
    return jnp.dot(a, b, preferred_element_type=F32)


def _dot_nt(a, b):
    return lax.dot_general(a, b, (((1,), (1,)), ((), ())), preferred_element_type=F32)


def _dot3_left(m_bf16, x):
    hi, mid, lo = _split3(x)
    return _dot(m_bf16, hi) + _dot(m_bf16, mid) + _dot(m_bf16, lo)


def _dot3_right(x, m_bf16):
    hi, mid, lo = _split3(x)
    return _dot(hi, m_bf16) + _dot(mid, m_bf16) + _dot(lo, m_bf16)


def _lower_tri(n):
    r = lax.broadcasted_iota(jnp.int32, (n, n), 0)
    c = lax.broadcasted_iota(jnp.int32, (n, n), 1)
    return r >= c


def _block_transpose(v):
    lane_blk = lax.broadcasted_iota(jnp.int32, v[0].shape, 1) // S5_GROUP
    b = GROUPS_PER_VREG // 2
    while b:
        bit_set = (lane_blk & b) != 0
        out = list(v)
        for r in range(GROUPS_PER_VREG):
            if r & b == 0:
                rp = r | b
                out[r] = jnp.where(bit_set, pltpu.roll(v[rp], b * S5_GROUP, 1), v[r])
                out[rp] = jnp.where(bit_set, v[rp], pltpu.roll(v[r], LANES - b * S5_GROUP, 1))
        v = out
        b //= 2
    return v


IN_SEGMENTS = (
    ("qk", 2 * ML_WIDTH, BF16),
    ("v", ML_WIDTH, BF16),
    ("o", ML_WIDTH, BF16),
    ("z", M2_WIDTH, BF16),
    ("xbc", M2_CONV_DIM, BF16),
    ("gates", N_BRANCH * D_MODEL, BF16),
    ("small", LANES, F32),
)


W_A_WIDTH = S5_WIDTH + 4 * ML_WIDTH


def _split_w_in(w_in):
    splits = (S5_WIDTH, ML_WIDTH, ML_WIDTH, ML_WIDTH, ML_WIDTH, ML_HEADS, ML_HEADS, M2_WIDTH, M2_CONV_DIM, M2_HEADS, N_BRANCH * D_MODEL)
    offs = np.cumsum((0,) + splits)
    w16 = w_in.astype(BF16)
    col = lambda n0, n1: w16[..., offs[n0]:offs[n1]]
    pad = jnp.zeros(w_in.shape[:-1] + (LANES - 2 * ML_HEADS - M2_HEADS,), BF16)
    small = jnp.concatenate([col(5, 7), col(9, 10), pad], axis=-1)
    return [w16, col(7, 8), col(8, 9), col(10, 11), small]


def _sigmoid(x):
    return 0.5 * jnp.tanh(0.5 * x) + 0.5


def _silu(x):
    h = 0.5 * x
    return h + h * jnp.tanh(h)


def _causal_conv(x_ref, tail_ref, w_ref, b_ref, first):
    n = x_ref.shape[0]
    x = x_ref[...].astype(F32)
    row = lax.broadcasted_iota(jnp.int32, (TAIL, 1), 0)
    shifts = [CONV_K - 1 - j for j in range(CONV_K - 1)]
    prev = jnp.where(first, 0.0, tail_ref[...].astype(F32))
    prev_rot = [pltpu.roll(prev, s, 0) for s in shifts]
    tiles = []
    for i in range(n // TAIL):
        cur = x[i * TAIL:(i + 1) * TAIL, :]
        cur_rot = [pltpu.roll(cur, s, 0) for s in shifts]
        acc = b_ref[...] + w_ref[CONV_K - 1:CONV_K, :] * cur
        for j, s in enumerate(shifts):
            acc = acc + w_ref[j:j + 1, :] * jnp.where(row < s, prev_rot[j], cur_rot[j])
        tiles.append(acc)
        prev_rot = cur_rot
    return jnp.concatenate(tiles, axis=0)


def _in_proj_kernel(x_ref, nw_ref, wa_ref, wz_ref, wxbc_ref, wg_ref, wsmall_ref,
                    ug_ref, qk_ref, v_ref, o_ref, z_ref, xbc_ref, g_ref, small_ref, u_scr):
    hb = _rms(x_ref[...], nw_ref[...]).astype(BF16)
    u = _dot(hb, wa_ref[:, 0:S5_WIDTH])
    off = S5_WIDTH
    for ref in (qk_ref, v_ref, o_ref):
        width = ref.shape[1]
        ref[...] = _dot(hb, wa_ref[:, off:off + width]).astype(ref.dtype)
        off += width
    for ref, w_ref in ((z_ref, wz_ref), (xbc_ref, wxbc_ref), (g_ref, wg_ref), (small_ref, wsmall_ref)):
        ref[...] = _dot(hb, w_ref[...]).astype(ref.dtype)

    nchunks = u_scr.shape[1] // S5_CHUNK
    for jb in range(S5_GROUPS // GROUPS_PER_VREG):
        u_scr[jb] = u[:, jb * LANES:(jb + 1) * LANES]
        for half in range(S5_CHUNK // GROUPS_PER_VREG):
            v = [u_scr[jb, pl.ds(half * GROUPS_PER_VREG + t, nchunks, stride=S5_CHUNK), :] for t in range(GROUPS_PER_VREG)]
            v = _block_transpose(v)
            for g in range(GROUPS_PER_VREG):
                ug_ref[jb * GROUPS_PER_VREG + g, :, half * LANES:(half + 1) * LANES] = v[g].astype(ug_ref.dtype)


def _in_proj(x2, norm_w, w_parts, layer):
    t = x2.shape[0]
    tm = min(TOKEN_BLOCK, t)
    nchunks = tm // S5_CHUNK
    return pl.pallas_call(
        _in_proj_kernel,
        grid=(t // tm,),
        in_specs=[pl.BlockSpec((tm, D_MODEL), lambda i: (i, 0)), _const_spec((1, D_MODEL)),
                  pl.BlockSpec((None, D_MODEL, W_A_WIDTH), lambda i: (layer, 0, 0), pipeline_mode=pl.Buffered(1))]
        + [_layer_spec(w, layer) for w in w_parts[1:]],
        out_specs=[pl.BlockSpec((S5_GROUPS, nchunks, S5_K), lambda i: (0, i, 0))]
        + [pl.BlockSpec((tm, w), lambda i: (i, 0)) for _, w, _ in IN_SEGMENTS],
        out_shape=[jax.ShapeDtypeStruct((S5_GROUPS, t // S5_CHUNK, S5_K), BF16)]
        + [jax.ShapeDtypeStruct((t, w), dt) for _, w, dt in IN_SEGMENTS],
        scratch_shapes=[pltpu.VMEM((S5_WIDTH // LANES, tm, LANES), F32)],
        compiler_params=_cparams("parallel"),
        name="in_proj",
    )(x2, norm_w.reshape(1, D_MODEL), *w_parts)


def _s5_prep_kernel(*refs):
    for gi in range(refs[0].shape[0]):
        _s5_prep_group(*[r.at[gi:gi + 1] for r in refs])


def _s5_prep_group(lr_ref, li_ref, logdt_ref, bt_ref, bti_ref, btr_t_ref, bti_t_ref, cr_ref, ci_ref, d_ref,
                   tt_ref, w_ref, mt_ref, pq_ref):
    p64 = S5_STATE
    lr, li = lr_ref[0], li_ref[0]
    dt = jnp.exp(logdt_ref[0])
    mag = jnp.exp(lr * dt)
    lb_re = mag * jnp.cos(li * dt)
    lb_im = mag * jnp.sin(li * dt)
    den = lr * lr + li * li
    f_re = ((lb_re - 1.0) * lr + lb_im * li) / den
    f_im = (lb_im * lr - (lb_re - 1.0) * li) / den

    pw = [(jnp.ones_like(lb_re), jnp.zeros_like(lb_im))]
    for _ in range(S5_CHUNK):
        pr, pi = pw[-1]
        pw.append((pr * lb_re - pi * lb_im, pr * lb_im + pi * lb_re))

    cr, ci = cr_ref[0], ci_ref[0]
    btr_t, bti_t = btr_t_ref[0], bti_t_ref[0]
    g_re, g_im, m_re, m_im, w_re, w_im = [], [], [], [], [], []
    for k in range(S5_CHUNK):
        pr, pi = pw[k]
        afr = pr * f_re - pi * f_im
        afi = pr * f_im + pi * f_re
        g_re.append(afr * cr - afi * ci)
        g_im.append(afr * ci + afi * cr)
        qr, qi = pw[k + 1]
        m_re.append(qr * cr - qi * ci)
        m_im.append(qr * ci + qi * cr)
        sr, si = pw[S5_CHUNK - 1 - k]
        wfr = sr * f_re - si * f_im
        wfi = sr * f_im + si * f_re
        w_re.append(wfr * btr_t - wfi * bti_t)
        w_im.append(wfr * bti_t + wfi * btr_t)
    cat = lambda blocks: jnp.concatenate(blocks, axis=0)
    w_ref[0, :, 0:p64] = cat(w_re).astype(w_ref.dtype)
    w_ref[0, :, p64:2 * p64] = cat(w_im).astype(w_ref.dtype)
    mt_ref[0, :, 0:p64] = cat(m_re).astype(mt_ref.dtype)
    mt_ref[0, :, p64:2 * p64] = (-cat(m_im)).astype(mt_ref.dtype)

    hi = lax.Precision.HIGHEST
    kk = (jnp.dot(cat(g_re), bt_ref[0], precision=hi, preferred_element_type=F32)
          - jnp.dot(cat(g_im), bti_ref[0], precision=hi, preferred_element_type=F32))
    rows = lax.broadcasted_iota(jnp.int32, (S5_K, S5_K), 0)
    cols = lax.broadcasted_iota(jnp.int32, (S5_K, S5_K), 1)
    col_blk = cols // S5_GROUP
    tt = jnp.where(rows == cols, d_ref[0], 0.0)
    for s in range(S5_CHUNK):
        if s == 0:
            shifted = kk
        else:
            shifted = jnp.concatenate([jnp.zeros((s * S5_GROUP, S5_K), F32), kk[:S5_K - s * S5_GROUP, :]], axis=0)
        tt = tt + jnp.where(col_blk == s, shifted, 0.0)
    tt_ref[0] = tt.astype(tt_ref.dtype)

    cr16, ci16 = pw[S5_CHUNK]
    pq_ref[0, 0:1, :] = jnp.concatenate([cr16, cr16], axis=1)
    pq_ref[0, 1:2, :] = jnp.concatenate([-ci16, ci16], axis=1)


def _s5_prep(a_re, a_im, b_re, b_im, c_re, c_im, d_skip, log_dt):
    g, p = S5_GROUPS, S5_STATE
    tile_lanes = lambda b: jnp.tile(b, (1, 1, S5_CHUNK))
    swap = lambda b: jnp.transpose(b, (0, 2, 1))
    args = (
        a_re.reshape(g, 1, p), a_im.reshape(g, 1, p), log_dt.reshape(g, 1, 1),
        tile_lanes(b_re), tile_lanes(b_im), swap(b_re), swap(b_im), c_re, c_im,
        jnp.tile(d_skip, (1, S5_CHUNK)).reshape(g, 1, S5_K),
    )
    spec = lambda a: pl.BlockSpec((S5_PREP_GROUPS,) + a.shape[1:], lambda i: (i, 0, 0))
    out_shapes = (
        jax.ShapeDtypeStruct((g, S5_K, S5_K), BF16),
        jax.ShapeDtypeStruct((g, S5_K, 2 * p), BF16),
        jax.ShapeDtypeStruct((g, S5_K, 2 * p), BF16),
        jax.ShapeDtypeStruct((g, 2, 2 * p), F32),
    )
    return pl.pallas_call(
        _s5_prep_kernel,
        grid=(g // S5_PREP_GROUPS,),
        in_specs=[spec(a) for a in args],
        out_specs=[spec(o) for o in out_shapes],
        out_shape=out_shapes,
        compiler_params=_cparams("parallel"),
        name="s5_prep",
    )(*args)


def _s5_kernel(ug_ref, tt_ref, w_ref, mt_ref, pq_ref, yg_ref, l_scr, s_scr, state_scr):
    ng = S5_GROUPS
    nchunks = ug_ref.shape[1]

    @pl.when(pl.program_id(1) == 0)
    def _():
        state_scr[...] = jnp.zeros_like(state_scr)

    def local(g, carry):
        l_scr[pl.ds(g, nchunks, stride=ng), :] = _dot(ug_ref[g], w_ref[g])
        return carry

    lax.fori_loop(0, ng, local, 0, unroll=16)

    p_mul = pq_ref[0]
    q_mul = pq_ref[1]

    def step(j, carry):
        x, xs = carry
        row = pl.multiple_of(j * ng, ng)
        s_scr[pl.ds(row, ng), :] = x
        l = l_scr[pl.ds(row, ng), :]
        ls = pltpu.roll(l, S5_STATE, 1)
        return p_mul * x + q_mul * xs + l, p_mul * xs - q_mul * x + ls

    x, xs = lax.fori_loop(0, nchunks, step, (state_scr[0], state_scr[1]), unroll=8)
    state_scr[0] = x
    state_scr[1] = xs

    def output(g, carry):
        u = ug_ref[g]
        s_in = s_scr[pl.ds(g, nchunks, stride=ng), :].astype(BF16)
        y = _dot_nt(u, tt_ref[g]) + _dot_nt(s_in, mt_ref[g])
        yg_ref[g] = jax.nn.gelu(y).astype(yg_ref.dtype)
        return carry

    lax.fori_loop(0, ng, output, 0, unroll=16)


def _s5_mixer(ug, bsz, mats):
    tt, w_mat, mt, pq = mats
    pq = jnp.transpose(pq, (1, 0, 2))
    g, nch, _ = ug.shape
    nch_b = nch // bsz
    cb = min(S5_CHUNK_BLOCK, nch_b)
    nblk = nch_b // cb
    blk = pl.BlockSpec((g, cb, S5_K), lambda b, j: (0, b * nblk + j, 0))
    return pl.pallas_call(
        _s5_kernel,
        grid=(bsz, nblk),
        in_specs=[blk, _const_spec(tt.shape), _const_spec(w_mat.shape), _const_spec(mt.shape), _const_spec(pq.shape)],
        out_specs=blk,
        out_shape=jax.ShapeDtypeStruct(ug.shape, BF16),
        scratch_shapes=[
            pltpu.VMEM((cb * g, 2 * S5_STATE), F32),
            pltpu.VMEM((cb * g, 2 * S5_STATE), F32),
            pltpu.VMEM((2, g, 2 * S5_STATE), F32),
        ],
        compiler_params=_cparams("arbitrary", "arbitrary"),
        name="s5",
    )(ug, tt, w_mat, mt, pq)


def _mlstm_kernel(qk_ref, tail_ref, v_ref, o_ref, small_ref, cw_ref, cb_ref, gbias_ref, y_ref, ct_scr, m_scr):
    first = pl.program_id(0) == 0
    n = qk_ref.shape[0]
    dh = ML_HEAD_DIM

    @pl.when(first)
    def _():
        ct_scr[...] = jnp.zeros_like(ct_scr)
        m_scr[...] = jnp.zeros_like(m_scr)

    ct_in = [ct_scr[h] for h in range(ML_HEADS)]
    m_in = [m_scr[h][:, 0:1] for h in range(ML_HEADS)]
    qk = _silu(_causal_conv(qk_ref, tail_ref, cw_ref, cb_ref, first))
    gate = small_ref[...] + gbias_ref[...]
    lf = jax.nn.log_sigmoid(gate)
    tri = _lower_tri(n)
    b_all = _dot3_left(tri.astype(BF16), lf)
    gate_t = gate.T
    b_t = b_all.T
    ones_col = (lax.broadcasted_iota(jnp.int32, (n, dh), 1) == 0).astype(BF16)

    outs, ct_out, m_out = [], [], []
    for h in range(ML_HEADS):
        q = qk[:, h * dh:(h + 1) * dh].astype(BF16)
        k = qk[:, ML_WIDTH + h * dh:ML_WIDTH + (h + 1) * dh] * (dh ** -0.5)
        k_t = k.T.astype(BF16)
        v_ext = jnp.concatenate([v_ref[:, h * dh:(h + 1) * dh], ones_col], axis=1)
        b_col = b_all[:, SLOT_F + h:SLOT_F + h + 1]
        ig_col = gate[:, SLOT_I + h:SLOT_I + h + 1]
        b_row = b_t[SLOT_F + h:SLOT_F + h + 1, :]
        ig_row = gate_t[SLOT_I + h:SLOT_I + h + 1, :]
        m_prev = m_in[h]
        ct_prev = ct_in[h]

        dmat = jnp.where(tri, b_col - b_row + ig_row, -jnp.inf)
        inter = b_col + m_prev
        m_t = jnp.maximum(jnp.max(dmat, axis=1, keepdims=True), inter)
        p = _dot(q, k_t) * jnp.exp(dmat - m_t)
        w_inter = jnp.exp(inter - m_t)
        num_den = _dot(p.astype(BF16), v_ext) + w_inter * _dot(q, ct_prev.astype(BF16))
        den = num_den[:, dh:dh + 1]
        denom = jnp.maximum(jnp.abs(den), jnp.exp(-m_t))
        out = num_den[:, :dh] / denom * _sigmoid(o_ref[:, h * dh:(h + 1) * dh].astype(F32))
        outs.append(out.astype(y_ref.dtype))

        b_end = b_col[n - 1:n, :]
        a_end = b_end - b_col + ig_col
        m_loc = jnp.max(a_end, axis=0, keepdims=True)
        w_end = jnp.exp(a_end - m_loc)
        m_new = jnp.maximum(b_end + m_prev, m_loc)
        s_prev = jnp.exp(b_end + m_prev - m_new)
        s_loc = jnp.exp(m_loc - m_new)
        upd = _dot(k_t, (w_end * v_ext.astype(F32)).astype(BF16))
        ct_out.append(s_prev * ct_prev + s_loc * upd)
        m_out.append(jnp.broadcast_to(m_new, m_scr.shape[1:]))

    y_ref[...] = jnp.concatenate(outs, axis=1)
    for h in range(ML_HEADS):
        ct_scr[h] = ct_out[h]
        m_scr[h] = m_out[h]


def _seq_specs(n, bsz):
    row = lambda w: pl.BlockSpec((bsz, n, w), lambda c: (0, c, 0))
    tail = lambda w: pl.BlockSpec((bsz, TAIL, w), lambda c: (0, jnp.maximum(c * (n // TAIL) - 1, 0), 0))
    return row, tail


def _by_seq(a, bsz):
    return a.reshape(bsz, a.shape[0] // bsz, a.shape[1])


def _mlstm_call_parts(qk, v, o, small, conv_w, conv_b, gbias, n, bsz):
    row, tail = _seq_specs(n, bsz)
    qk, v, o, small = (_by_seq(a, bsz) for a in (qk, v, o, small))
    in_specs = [row(2 * ML_WIDTH), tail(2 * ML_WIDTH), row(ML_WIDTH), row(ML_WIDTH), row(LANES),
                _const_spec((CONV_K, 2 * ML_WIDTH)), _const_spec((1, 2 * ML_WIDTH)), _const_spec((1, LANES))]
    args = (qk, qk, v, o, small, conv_w, conv_b.reshape(1, -1), gbias)
    scratch = [pltpu.VMEM((bsz, ML_HEADS, ML_HEAD_DIM, 2 * ML_HEAD_DIM), F32),
               pltpu.VMEM((bsz, ML_HEADS, 1, LANES), F32)]
    return in_specs, args, 5, row(ML_WIDTH), jax.ShapeDtypeStruct(qk.shape[:2] + (ML_WIDTH,), BF16), scratch


def _ssd_kernel(z_ref, xbc_ref, tail_ref, small_ref, cw_ref, cb_ref, gbias_ref, alog_ref, dskip_ref, normw_ref, expand_ref,
                y_ref, state_scr):
    first = pl.program_id(0) == 0
    n = z_ref.shape[0]
    gw = M2_GROUP_WIDTH
    hd = M2_HEAD_DIM

    @pl.when(first)
    def _():
        state_scr[...] = jnp.zeros_like(state_scr)

    state_in = [state_scr[g] for g in range(M2_GROUPS)]
    xbc = _silu(_causal_conv(xbc_ref, tail_ref, cw_ref, cb_ref, first))
    xs = xbc[:, :M2_WIDTH]
    lane = lax.broadcasted_iota(jnp.int32, (1, LANES), 1)
    dt_lanes = (lane >= SLOT_DT) & (lane < SLOT_DT + M2_HEADS)
    dt = jnp.where(dt_lanes, jax.nn.softplus(small_ref[...] + gbias_ref[...]), 0.0)
    a_row = jnp.where(dt_lanes, -jnp.exp(alog_ref[...]), 0.0)
    tri = _lower_tri(n)
    cum = _dot3_left(tri.astype(BF16), dt * a_row)
    cum_t = cum.T
    cum_end = cum[n - 1:n, :]
    expand = expand_ref[...]
    dt_e = _dot(dt.astype(BF16), expand)
    decay_in = _dot(jnp.exp(cum).astype(BF16), expand)
    decay_out = _dot((dt * jnp.exp(cum_end - cum)).astype(BF16), expand)
    decay_chunk = _dot3_right(jnp.exp(cum[n - SUBLANES:n, :]), expand)[SUBLANES - 1:SUBLANES, :]
    xdt_b = (xs * dt_e).astype(BF16)
    x_in = (xs * decay_out).astype(BF16)
    left_half = lax.broadcasted_iota(jnp.int32, (n, 2 * hd), 1) < hd

    y_out, state_out = [], []
    for g in range(M2_GROUPS):
        b_g = xbc[:, M2_WIDTH + g * M2_STATE:M2_WIDTH + (g + 1) * M2_STATE]
        c_g = xbc[:, M2_WIDTH + (M2_GROUPS + g) * M2_STATE:M2_WIDTH + (M2_GROUPS + g + 1) * M2_STATE].astype(BF16)
        b_t = b_g.T.astype(BF16)
        cb = _dot(c_g, b_t)
        y_diag = []
        for pair in range(M2_HPG // 2):
            head = g * M2_HPG + 2 * pair
            ws = []
            for hh in (head, head + 1):
                cum_col = cum[:, SLOT_DT + hh:SLOT_DT + hh + 1]
                cum_row = cum_t[SLOT_DT + hh:SLOT_DT + hh + 1, :]
                ws.append((cb * jnp.exp(jnp.where(tri, cum_col - cum_row, -jnp.inf))).astype(BF16))
            xp = xdt_b[:, head * hd:(head + 2) * hd]
            zero = jnp.zeros_like(xp)
            rhs = jnp.concatenate([jnp.where(left_half, xp, zero), jnp.where(left_half, zero, xp)], axis=0)
            y_diag.append(_dot(jnp.concatenate(ws, axis=1), rhs))
        cols = slice(g * gw, (g + 1) * gw)
        st = state_in[g]
        y_off = _dot(c_g, st.astype(BF16)) * decay_in[:, cols]
        y = jnp.concatenate(y_diag, axis=1) + y_off + dskip_ref[:, cols] * xs[:, cols]
        y = y * _silu(z_ref[:, cols].astype(F32))
        y = y * lax.rsqrt(jnp.mean(y * y, axis=-1, keepdims=True) + EPS)
        y_out.append((y * normw_ref[:, cols]).astype(y_ref.dtype))
        state_out.append(st * decay_chunk[:, cols] + _dot(b_t, x_in[:, cols]))

    y_ref[...] = jnp.concatenate(y_out, axis=1)
    for g in range(M2_GROUPS):
        state_scr[g] = state_out[g]


def _ssd_call_parts(z, xbc, small, conv_w, conv_b, gbias, alog_row, dskip_row, norm_w, n, bsz):
    row, tail = _seq_specs(n, bsz)
    z, xbc, small = (_by_seq(a, bsz) for a in (z, xbc, small))
    expand = np.zeros((LANES, M2_WIDTH), np.float32)
    for h in range(M2_HEADS):
        expand[SLOT_DT + h, h * M2_HEAD_DIM:(h + 1) * M2_HEAD_DIM] = 1.0
    in_specs = [row(M2_WIDTH), row(M2_CONV_DIM), tail(M2_CONV_DIM), row(LANES),
                _const_spec((CONV_K, M2_CONV_DIM)), _const_spec((1, M2_CONV_DIM)), _const_spec((1, LANES)), _const_spec((1, LANES)),
                _const_spec((1, M2_WIDTH)), _const_spec((1, M2_WIDTH)), _const_spec((LANES, M2_WIDTH))]
    args = (z, xbc, xbc, small, conv_w, conv_b.reshape(1, -1), gbias, alog_row, dskip_row, norm_w.reshape(1, -1),
            jnp.asarray(expand, BF16))
    scratch = [pltpu.VMEM((bsz, M2_GROUPS, M2_STATE, M2_GROUP_WIDTH), F32)]
    return in_specs, args, 4, row(M2_WIDTH), jax.ShapeDtypeStruct(z.shape[:2] + (M2_WIDTH,), BF16), scratch


def _mixers(ml_inputs, ssd_inputs, bsz):
    t = ml_inputs[0].shape[0]
    seq = t // bsz
    n = min(SEQ_CHUNK, seq)
    ml_specs, ml_args, ml_nseq, ml_out_spec, ml_out_shape, ml_scratch = _mlstm_call_parts(*ml_inputs, n, bsz)
    ssd_specs, ssd_args, ssd_nseq, ssd_out_spec, ssd_out_shape, ssd_scratch = _ssd_call_parts(*ssd_inputs, n, bsz)
    n_ml, n_ssd = len(ml_args), len(ssd_args)

    def body(*refs):
        ins, outs, scr = refs[:n_ml + n_ssd], refs[n_ml + n_ssd:n_ml + n_ssd + 2], refs[n_ml + n_ssd + 2:]
        for b in range(bsz):
            per_seq = lambda rs, k: [r.at[b] if i < k else r for i, r in enumerate(rs)]
            _mlstm_kernel(*per_seq(ins[:n_ml], ml_nseq), outs[0].at[b], *[s.at[b] for s in scr[:len(ml_scratch)]])
            _ssd_kernel(*per_seq(ins[n_ml:], ssd_nseq), outs[1].at[b], *[s.at[b] for s in scr[len(ml_scratch):]])

    yb, yc = pl.pallas_call(
        body,
        grid=(seq // n,),
        in_specs=ml_specs + ssd_specs,
        out_specs=[ml_out_spec, ssd_out_spec],
        out_shape=[ml_out_shape, ssd_out_shape],
        scratch_shapes=ml_scratch + ssd_scratch,
        compiler_params=_cparams("arbitrary"),
        name="mixers",
    )(*ml_args, *ssd_args)
    return yb.reshape(t, ML_WIDTH), yc.reshape(t, M2_WIDTH)


def _merge_kernel(x_ref, yg_ref, yb_ref, yc_ref, g_ref, wglu_ref, wa_ref, wb_ref, wc_ref, wout_ref, nmp_ref, out_ref, ys_scr):
    d = D_MODEL
    gate = lambda i: _sigmoid(g_ref[:, i * d:(i + 1) * d].astype(F32))
    merged_bc = gate(1) * _dot(yb_ref[...], wb_ref[...]) + gate(2) * _dot(yc_ref[...], wc_ref[...])
    nchunks = yg_ref.shape[1]
    for jb in range(S5_GROUPS // GROUPS_PER_VREG):
        for half in range(S5_CHUNK // GROUPS_PER_VREG):
            v = [yg_ref[jb * GROUPS_PER_VREG + g, :, half * LANES:(half + 1) * LANES].astype(F32) for g in range(GROUPS_PER_VREG)]
            v = _block_transpose(v)
            for t in range(GROUPS_PER_VREG):
                ys_scr[jb, pl.ds(half * GROUPS_PER_VREG + t, nchunks, stride=S5_CHUNK), :] = v[t]
    ys5 = jnp.concatenate([ys_scr[jb] for jb in range(S5_WIDTH // LANES)], axis=1)
    ya = ys5 * _sigmoid(_dot(ys5.astype(BF16), wglu_ref[...]))
    merged = gate(0) * _dot(ya.astype(BF16), wa_ref[...]) + merged_bc
    mix = _dot(merged.astype(BF16), wout_ref[...])
    out_ref[...] = x_ref[...] + _rms(mix, nmp_ref[...])


def _merge(x2, yg, yb, yc, gates, weights, norm_post, layer):
    t = x2.shape[0]
    tm = min(WIDE_TOKEN_BLOCK, t)
    row = lambda w: pl.BlockSpec((tm, w), lambda i: (i, 0))
    return pl.pallas_call(
        _merge_kernel,
        grid=(t // tm,),
        in_specs=[row(D_MODEL), pl.BlockSpec((S5_GROUPS, tm // S5_CHUNK, S5_K), lambda i: (0, i, 0)),
                  row(ML_WIDTH), row(M2_WIDTH), row(N_BRANCH * D_MODEL)]
        + [_layer_spec(w, layer) for w in weights] + [_const_spec((1, D_MODEL))],
        out_specs=row(D_MODEL),
        out_shape=jax.ShapeDtypeStruct((t, D_MODEL), F32),
        scratch_shapes=[pltpu.VMEM((S5_WIDTH // LANES, tm, LANES), F32)],
        compiler_params=_cparams("parallel"),
        name="merge",
    )(x2, yg, yb, yc, gates, *weights, norm_post.reshape(1, D_MODEL))


def _ffn_kernel(x_ref, npre_ref, npost_ref, wg_ref, wu_ref, wd_ref, out_ref):
    x = x_ref[...]
    hb = _rms(x, npre_ref[...]).astype(BF16)
    act = _silu(_dot(hb, wg_ref[...])) * _dot(hb, wu_ref[...])
    ffn = _dot(act.astype(BF16), wd_ref[...])
    out_ref[...] = x + _rms(ffn, npost_ref[...])


def _ffn(x2, weights, norm_pre, norm_post, layer):
    t = x2.shape[0]
    tm = min(WIDE_TOKEN_BLOCK, t)
    row = pl.BlockSpec((tm, D_MODEL), lambda i: (i, 0))
    return pl.pallas_call(
        _ffn_kernel,
        grid=(t // tm,),
        in_specs=[row, _const_spec((1, D_MODEL)), _const_spec((1, D_MODEL))] + [_layer_spec(w, layer) for w in weights],
        out_specs=row,
        out_shape=jax.ShapeDtypeStruct((t, D_MODEL), F32),
        compiler_params=_cparams("parallel"),
        name="ffn",
    )(x2, norm_pre.reshape(1, D_MODEL), norm_post.reshape(1, D_MODEL), *weights)


def _gate_bias_row(bias_i, bias_f, dt_bias):
    pad = jnp.zeros((LANES - 2 * ML_HEADS - M2_HEADS,), F32)
    return jnp.concatenate([bias_i, bias_f, dt_bias, pad]).reshape(1, LANES)


def _head_lane_row(v):
    pad_l = jnp.zeros((SLOT_DT,), F32)
    pad_r = jnp.zeros((LANES - SLOT_DT - M2_HEADS,), F32)
    return jnp.concatenate([pad_l, v, pad_r]).reshape(1, LANES)


def _layer(x2, bsz, p, big, layer):
    ug, qk, v, o, z, xbc, gates, small = _in_proj(x2, p["norm_mix_pre"], big["w_in"], layer)
    mats = _s5_prep(p["s5_a_re"], p["s5_a_im"], p["s5_b_re"], p["s5_b_im"], p["s5_c_re"], p["s5_c_im"], p["s5_d"], p["s5_log_dt"])
    yg = _s5_mixer(ug, bsz, mats)
    gbias = _gate_bias_row(p["ml_bias_i"], p["ml_bias_f"], p["m2_dt_bias"])
    yb, yc = _mixers(
        (qk, v, o, small, p["ml_conv_w"], p["ml_conv_b"], gbias),
        (z, xbc, small, p["m2_conv_w"], p["m2_conv_b"], gbias, _head_lane_row(p["m2_a_log"]),
         jnp.repeat(p["m2_d"], M2_HEAD_DIM).reshape(1, M2_WIDTH), p["m2_norm_w"]),
        bsz)
    x1 = _merge(x2, yg, yb, yc, gates, big["merge"], p["norm_mix_post"], layer)
    return _ffn(x1, big["ffn"], p["norm_ffn_pre"], p["norm_ffn_post"], layer)


_PARAM_NAMES = ("norm_mix_pre", "norm_mix_post", "w_in", "s5_a_re", "s5_a_im", "s5_b_re", "s5_b_im", "s5_c_re", "s5_c_im", "s5_d",
                "s5_log_dt", "s5_w_glu", "ml_conv_w", "ml_conv_b", "ml_bias_i", "ml_bias_f", "m2_conv_w", "m2_conv_b", "m2_dt_bias",
                "m2_a_log", "m2_d", "m2_norm_w", "w_br_a", "w_br_b", "w_br_c", "w_out", "norm_ffn_pre", "norm_ffn_post",
                "w_ffn_gate", "w_ffn_up", "w_ffn_down")


def kernel(x, norm_mix_pre, norm_mix_post, w_in, s5_a_re, s5_a_im, s5_b_re, s5_b_im, s5_c_re, s5_c_im, s5_d, s5_log_dt, s5_w_glu, ml_conv_w, ml_conv_b, ml_bias_i, ml_bias_f, m2_conv_w, m2_conv_b, m2_dt_bias, m2_a_log, m2_d, m2_norm_w, w_br_a, w_br_b, w_br_c, w_out, norm_ffn_pre, norm_ffn_post, w_ffn_gate, w_ffn_up, w_ffn_down):
    stacked = (norm_mix_pre, norm_mix_post, w_in, s5_a_re, s5_a_im, s5_b_re, s5_b_im, s5_c_re, s5_c_im, s5_d, s5_log_dt, s5_w_glu,
               ml_conv_w, ml_conv_b, ml_bias_i, ml_bias_f, m2_conv_w, m2_conv_b, m2_dt_bias, m2_a_log, m2_d, m2_norm_w,
               w_br_a, w_br_b, w_br_c, w_out, norm_ffn_pre, norm_ffn_post, w_ffn_gate, w_ffn_up, w_ffn_down)
    bsz, seq, d = x.shape
    x2 = x.reshape(bsz * seq, d)
    big = {
        "w_in": _split_w_in(w_in),
        "merge": [w.astype(BF16) for w in (s5_w_glu, w_br_a, w_br_b, w_br_c, w_out)],
        "ffn": [w.astype(BF16) for w in (w_ffn_gate, w_ffn_up, w_ffn_down)],
    }
    for layer in range(norm_mix_pre.shape[0]):
        x2 = _layer(x2, bsz, {name: arr[layer] for name, arr in zip(_PARAM_NAMES, stacked)}, big, layer)
    return x2.reshape(bsz, seq, d)
```

```python
import jax
import jax.numpy as jnp
import numpy as np
from jax import lax
from jax.experimental import pallas as pl
from jax.experimental.pallas import tpu as pltpu

F32 = jnp.float32
BF16 = jnp.bfloat16

D_MODEL = 1024
EPS = 1e-6
CONV_K = 4

S5_WIDTH = 512
S5_GROUP = 16
S5_GROUPS = S5_WIDTH // S5_GROUP
S5_STATE = 64
S5_CHUNK = 16
S5_K = S5_CHUNK * S5_GROUP
S5_CHUNK_BLOCK = 256
S5_PREP_GROUPS = 4

ML_WIDTH = 512
ML_HEADS = 4
ML_HEAD_DIM = ML_WIDTH // ML_HEADS

M2_WIDTH = 1024
M2_HEAD_DIM = 64
M2_HEADS = M2_WIDTH // M2_HEAD_DIM
M2_GROUPS = 2
M2_HPG = M2_HEADS // M2_GROUPS
M2_STATE = 128
M2_CONV_DIM = M2_WIDTH + 2 * M2_GROUPS * M2_STATE
M2_GROUP_WIDTH = M2_WIDTH // M2_GROUPS

N_BRANCH = 3
FFN_HIDDEN = -(-(8 * D_MODEL) // (3 * 256)) * 256

LANES = 128
SUBLANES = 8
TAIL = SUBLANES
GROUPS_PER_VREG = LANES // S5_GROUP

SLOT_I = 0
SLOT_F = ML_HEADS
SLOT_DT = 2 * ML_HEADS

SEQ_CHUNK = 256
TOKEN_BLOCK = 512

VMEM_LIMIT = 56 * 1024 * 1024


def _cparams(*sem):
    return pltpu.CompilerParams(dimension_semantics=sem, vmem_limit_bytes=VMEM_LIMIT)


def _const_spec(shape):
    nd = len(shape)
    return pl.BlockSpec(shape, lambda *_: (0,) * nd, pipeline_mode=pl.Buffered(1))


def _layer_spec(stacked, layer):
    shape = stacked.shape[1:]
    zeros = (0,) * len(shape)
    return pl.BlockSpec((None,) + shape, lambda *_: (layer,) + zeros, pipeline_mode=pl.Buffered(1))


def _rms(x, w):
    return x * lax.rsqrt(jnp.mean(x * x, axis=-1, keepdims=True) + EPS) * w


def _split3(x):
    hi = x.astype(BF16)
    r1 = x - hi.astype(F32)
    mid = r1.astype(BF16)
    lo = (r1 - mid.astype(F32)).astype(BF16)
    return hi, mid, lo


def _dot(a, b):
    return jnp.dot(a, b, preferred_element_type=F32)


def _dot_nt(a, b):
    return lax.dot_general(a, b, (((1,), (1,)), ((), ())), preferred_element_type=F32)


def _dot3_left(m_bf16, x):
    hi, mid, lo = _split3(x)
    return _dot(m_bf16, hi) + _dot(m_bf16, mid) + _dot(m_bf16, lo)


def _dot3_right(x, m_bf16):
    hi, mid, lo = _split3(x)
    return _dot(hi, m_bf16) + _dot(mid, m_bf16) + _dot(lo, m_bf16)


def _lower_tri(n):
    r = lax.broadcasted_iota(jnp.int32, (n, n), 0)
    c = lax.broadcasted_iota(jnp.int32, (n, n), 1)
    return r >= c


def _block_transpose(v):
    lane_blk = lax.broadcasted_iota(jnp.int32, v[0].shape, 1) // S5_GROUP
    b = GROUPS_PER_VREG // 2
    while b:
        bit_set = (lane_blk & b) != 0
        out = list(v)
        for r in range(GROUPS_PER_VREG):
            if r & b == 0:
                rp = r | b
                out[r] = jnp.where(bit_set, pltpu.roll(v[rp], b * S5_GROUP, 1), v[r])
                out[rp] = jnp.where(bit_set, v[rp], pltpu.roll(v[r], LANES - b * S5_GROUP, 1))
        v = out
        b //= 2
    return v


IN_SEGMENTS = (
    ("qk", 2 * ML_WIDTH, BF16),
    ("v", ML_WIDTH, BF16),
    ("o", ML_WIDTH, BF16),
    ("z", M2_WIDTH, BF16),
    ("xbc", M2_CONV_DIM, BF16),
    ("gates", N_BRANCH * D_MODEL, BF16),
    ("small", LANES, F32),
)


W_A_WIDTH = S5_WIDTH + 4 * ML_WIDTH


def _split_w_in(w_in):
    splits = (S5_WIDTH, ML_WIDTH, ML_WIDTH, ML_WIDTH, ML_WIDTH, ML_HEADS, ML_HEADS, M2_WIDTH, M2_CONV_DIM, M2_HEADS, N_BRANCH * D_MODEL)
    offs = np.cumsum((0,) + splits)
    w16 = w_in.astype(BF16)
    col = lambda n0, n1: w16[..., offs[n0]:offs[n1]]
    pad = jnp.zeros(w_in.shape[:-1] + (LANES - 2 * ML_HEADS - M2_HEADS,), BF16)
    small = jnp.concatenate([col(5, 7), col(9, 10), pad], axis=-1)
    return [w16, col(7, 8), col(8, 9), col(10, 11), small]


def _sigmoid(x):
    return 0.5 * jnp.tanh(0.5 * x) + 0.5


def _silu(x):
    h = 0.5 * x
    return h + h * jnp.tanh(h)


def _causal_conv(x_ref, tail_ref, w_ref, b_ref, first):
    n = x_ref.shape[0]
    x = x_ref[...].astype(F32)
    row = lax.broadcasted_iota(jnp.int32, (TAIL, 1), 0)
    shifts = [CONV_K - 1 - j for j in range(CONV_K - 1)]
    prev = jnp.where(first, 0.0, tail_ref[...].astype(F32))
    prev_rot = [pltpu.roll(prev, s, 0) for s in shifts]
    tiles = []
    for i in range(n // TAIL):
        cur = x[i * TAIL:(i + 1) * TAIL, :]
        cur_rot = [pltpu.roll(cur, s, 0) for s in shifts]
        acc = b_ref[...] + w_ref[CONV_K - 1:CONV_K, :] * cur
        for j, s in enumerate(shifts):
            acc = acc + w_ref[j:j + 1, :] * jnp.where(row < s, prev_rot[j], cur_rot[j])
        tiles.append(acc)
        prev_rot = cur_rot
    return jnp.concatenate(tiles, axis=0)


def _in_proj_kernel(x_ref, nw_ref, wa_ref, wz_ref, wxbc_ref, wg_ref, wsmall_ref,
                    ug_ref, qk_ref, v_ref, o_ref, z_ref, xbc_ref, g_ref, small_ref, u_scr):
    hb = _rms(x_ref[...], nw_ref[...]).astype(BF16)
    u = _dot(hb, wa_ref[:, 0:S5_WIDTH])
    off = S5_WIDTH
    for ref in (qk_ref, v_ref, o_ref):
        width = ref.shape[1]
        ref[...] = _dot(hb, wa_ref[:, off:off + width]).astype(ref.dtype)
        off += width
    for ref, w_ref in ((z_ref, wz_ref), (xbc_ref, wxbc_ref), (g_ref, wg_ref), (small_ref, wsmall_ref)):
        ref[...] = _dot(hb, w_ref[...]).astype(ref.dtype)

    nchunks = u_scr.shape[1] // S5_CHUNK
    for jb in range(S5_GROUPS // GROUPS_PER_VREG):
        u_scr[jb] = u[:, jb * LANES:(jb + 1) * LANES]
        for half in range(S5_CHUNK // GROUPS_PER_VREG):
            v = [u_scr[jb, pl.ds(half * GROUPS_PER_VREG + t, nchunks, stride=S5_CHUNK), :] for t in range(GROUPS_PER_VREG)]
            v = _block_transpose(v)
            for g in range(GROUPS_PER_VREG):
                ug_ref[jb * GROUPS_PER_VREG + g, :, half * LANES:(half + 1) * LANES] = v[g].astype(ug_ref.dtype)


def _in_proj(x2, norm_w, w_parts, layer):
    t = x2.shape[0]
    tm = min(TOKEN_BLOCK, t)
    nchunks = tm // S5_CHUNK
    return pl.pallas_call(
        _in_proj_kernel,
        grid=(t // tm,),
        in_specs=[pl.BlockSpec((tm, D_MODEL), lambda i: (i, 0)), _const_spec((1, D_MODEL)),
                  pl.BlockSpec((None, D_MODEL, W_A_WIDTH), lambda i: (layer, 0, 0), pipeline_mode=pl.Buffered(1))]
        + [_layer_spec(w, layer) for w in w_parts[1:]],
        out_specs=[pl.BlockSpec((S5_GROUPS, nchunks, S5_K), lambda i: (0, i, 0))]
        + [pl.BlockSpec((tm, w), lambda i: (i, 0)) for _, w, _ in IN_SEGMENTS],
        out_shape=[jax.ShapeDtypeStruct((S5_GROUPS, t // S5_CHUNK, S5_K), BF16)]
        + [jax.ShapeDtypeStruct((t, w), dt) for _, w, dt in IN_SEGMENTS],
        scratch_shapes=[pltpu.VMEM((S5_WIDTH // LANES, tm, LANES), F32)],
        compiler_params=_cparams("parallel"),
        name="in_proj",
    )(x2, norm_w.reshape(1, D_MODEL), *w_parts)


def _s5_prep_kernel(*refs):
    for gi in range(refs[0].shape[0]):
        _s5_prep_group(*[r.at[gi:gi + 1] for r in refs])


def _s5_prep_group(lr_ref, li_ref, logdt_ref, bt_ref, bti_ref, btr_t_ref, bti_t_ref, cr_ref, ci_ref, d_ref,
                   tt_ref, w_ref, mt_ref, pq_ref):
    p64 = S5_STATE
    lr, li = lr_ref[0], li_ref[0]
    dt = jnp.exp(logdt_ref[0])
    mag = jnp.exp(lr * dt)
    lb_re = mag * jnp.cos(li * dt)
    lb_im = mag * jnp.sin(li * dt)
    den = lr * lr + li * li
    f_re = ((lb_re - 1.0) * lr + lb_im * li) / den
    f_im = (lb_im * lr - (lb_re - 1.0) * li) / den

    pw = [(jnp.ones_like(lb_re), jnp.zeros_like(lb_im))]
    for _ in range(S5_CHUNK):
        pr, pi = pw[-1]
        pw.append((pr * lb_re - pi * lb_im, pr * lb_im + pi * lb_re))

    cr, ci = cr_ref[0], ci_ref[0]
    btr_t, bti_t = btr_t_ref[0], bti_t_ref[0]
    g_re, g_im, m_re, m_im, w_re, w_im = [], [], [], [], [], []
    for k in range(S5_CHUNK):
        pr, pi = pw[k]
        afr = pr * f_re - pi * f_im
        afi = pr * f_im + pi * f_re
        g_re.append(afr * cr - afi * ci)
        g_im.append(afr * ci + afi * cr)
        qr, qi = pw[k + 1]
        m_re.append(qr * cr - qi * ci)
        m_im.append(qr * ci + qi * cr)
        sr, si = pw[S5_CHUNK - 1 - k]
        wfr = sr * f_re - si * f_im
        wfi = sr * f_im + si * f_re
        w_re.append(wfr * btr_t - wfi * bti_t)
        w_im.append(wfr * bti_t + wfi * btr_t)
    cat = lambda blocks: jnp.concatenate(blocks, axis=0)
    w_ref[0, :, 0:p64] = cat(w_re).astype(w_ref.dtype)
    w_ref[0, :, p64:2 * p64] = cat(w_im).astype(w_ref.dtype)
    mt_ref[0, :, 0:p64] = cat(m_re).astype(mt_ref.dtype)
    mt_ref[0, :, p64:2 * p64] = (-cat(m_im)).astype(mt_ref.dtype)

    hi = lax.Precision.HIGHEST
    kk = (jnp.dot(cat(g_re), bt_ref[0], precision=hi, preferred_element_type=F32)
          - jnp.dot(cat(g_im), bti_ref[0], precision=hi, preferred_element_type=F32))
    rows = lax.broadcasted_iota(jnp.int32, (S5_K, S5_K), 0)
    cols = lax.broadcasted_iota(jnp.int32, (S5_K, S5_K), 1)
    col_blk = cols // S5_GROUP
    tt = jnp.where(rows == cols, d_ref[0], 0.0)
    for s in range(S5_CHUNK):
        if s == 0:
            shifted = kk
        else:
            shifted = jnp.concatenate([jnp.zeros((s * S5_GROUP, S5_K), F32), kk[:S5_K - s * S5_GROUP, :]], axis=0)
        tt = tt + jnp.where(col_blk == s, shifted, 0.0)
    tt_ref[0] = tt.astype(tt_ref.dtype)

    cr16, ci16 = pw[S5_CHUNK]
    pq_ref[0, 0:1, :] = jnp.concatenate([cr16, cr16], axis=1)
    pq_ref[0, 1:2, :] = jnp.concatenate([-ci16, ci16], axis=1)


def _s5_prep(a_re, a_im, b_re, b_im, c_re, c_im, d_skip, log_dt):
    g, p = S5_GROUPS, S5_STATE
    tile_lanes = lambda b: jnp.tile(b, (1, 1, S5_CHUNK))
    swap = lambda b: jnp.transpose(b, (0, 2, 1))
    args = (
        a_re.reshape(g, 1, p), a_im.reshape(g, 1, p), log_dt.reshape(g, 1, 1),
        tile_lanes(b_re), tile_lanes(b_im), swap(b_re), swap(b_im), c_re, c_im,
        jnp.tile(d_skip, (1, S5_CHUNK)).reshape(g, 1, S5_K),
    )
    spec = lambda a: pl.BlockSpec((S5_PREP_GROUPS,) + a.shape[1:], lambda i: (i, 0, 0))
    out_shapes = (
        jax.ShapeDtypeStruct((g, S5_K, S5_K), BF16),
        jax.ShapeDtypeStruct((g, S5_K, 2 * p), BF16),
        jax.ShapeDtypeStruct((g, S5_K, 2 * p), BF16),
        jax.ShapeDtypeStruct((g, 2, 2 * p), F32),
    )
    return pl.pallas_call(
        _s5_prep_kernel,
        grid=(g // S5_PREP_GROUPS,),
        in_specs=[spec(a) for a in args],
        out_specs=[spec(o) for o in out_shapes],
        out_shape=out_shapes,
        compiler_params=_cparams("parallel"),
        name="s5_prep",
    )(*args)


def _s5_kernel(ug_ref, tt_ref, w_ref, mt_ref, pq_ref, yg_ref, l_scr, s_scr, state_scr):
    ng = S5_GROUPS
    nchunks = ug_ref.shape[1]

    @pl.when(pl.program_id(1) == 0)
    def _():
        state_scr[...] = jnp.zeros_like(state_scr)

    def local(g, carry):
        l_scr[pl.ds(g, nchunks, stride=ng), :] = _dot(ug_ref[g], w_ref[g])
        return carry

    lax.fori_loop(0, ng, local, 0, unroll=8)

    p_mul = pq_ref[0]
    q_mul = pq_ref[1]

    def step(j, carry):
        x, xs = carry
        row = pl.multiple_of(j * ng, ng)
        s_scr[pl.ds(row, ng), :] = x
        l = l_scr[pl.ds(row, ng), :]
        ls = pltpu.roll(l, S5_STATE, 1)
        return p_mul * x + q_mul * xs + l, p_mul * xs - q_mul * x + ls

    x, xs = lax.fori_loop(0, nchunks, step, (state_scr[0], state_scr[1]), unroll=8)
    state_scr[0] = x
    state_scr[1] = xs

    def output(g, carry):
        u = ug_ref[g]
        s_in = s_scr[pl.ds(g, nchunks, stride=ng), :].astype(BF16)
        y = _dot_nt(u, tt_ref[g]) + _dot_nt(s_in, mt_ref[g])
        yg_ref[g] = jax.nn.gelu(y).astype(yg_ref.dtype)
        return carry

    lax.fori_loop(0, ng, output, 0, unroll=8)


def _s5_mixer(ug, bsz, mats):
    tt, w_mat, mt, pq = mats
    pq = jnp.transpose(pq, (1, 0, 2))
    g, nch, _ = ug.shape
    nch_b = nch // bsz
    cb = min(S5_CHUNK_BLOCK, nch_b)
    nblk = nch_b // cb
    blk = pl.BlockSpec((g, cb, S5_K), lambda b, j: (0, b * nblk + j, 0))
    return pl.pallas_call(
        _s5_kernel,
        grid=(bsz, nblk),
        in_specs=[blk, _const_spec(tt.shape), _const_spec(w_mat.shape), _const_spec(mt.shape), _const_spec(pq.shape)],
        out_specs=blk,
        out_shape=jax.ShapeDtypeStruct(ug.shape, BF16),
        scratch_shapes=[
            pltpu.VMEM((cb * g, 2 * S5_STATE), F32),
            pltpu.VMEM((cb * g, 2 * S5_STATE), F32),
            pltpu.VMEM((2, g, 2 * S5_STATE), F32),
        ],
        compiler_params=_cparams("arbitrary", "arbitrary"),
        name="s5",
    )(ug, tt, w_mat, mt, pq)


def _mlstm_kernel(qk_ref, tail_ref, v_ref, o_ref, small_ref, cw_ref, cb_ref, gbias_ref, y_ref, ct_scr, m_scr):
    first = pl.program_id(0) == 0
    n = qk_ref.shape[0]
    dh = ML_HEAD_DIM

    @pl.when(first)
    def _():
        ct_scr[...] = jnp.zeros_like(ct_scr)
        m_scr[...] = jnp.zeros_like(m_scr)

    ct_in = [ct_scr[h] for h in range(ML_HEADS)]
    m_in = [m_scr[h][:, 0:1] for h in range(ML_HEADS)]
    qk = _silu(_causal_conv(qk_ref, tail_ref, cw_ref, cb_ref, first))
    gate = small_ref[...] + gbias_ref[...]
    lf = jax.nn.log_sigmoid(gate)
    tri = _lower_tri(n)
    b_all = _dot3_left(tri.astype(BF16), lf)
    gate_t = gate.T
    b_t = b_all.T
    ones_col = (lax.broadcasted_iota(jnp.int32, (n, dh), 1) == 0).astype(BF16)

    outs, ct_out, m_out = [], [], []
    for h in range(ML_HEADS):
        q = qk[:, h * dh:(h + 1) * dh].astype(BF16)
        k = qk[:, ML_WIDTH + h * dh:ML_WIDTH + (h + 1) * dh] * (dh ** -0.5)
        k_t = k.T.astype(BF16)
        v_ext = jnp.concatenate([v_ref[:, h * dh:(h + 1) * dh], ones_col], axis=1)
        b_col = b_all[:, SLOT_F + h:SLOT_F + h + 1]
        ig_col = gate[:, SLOT_I + h:SLOT_I + h + 1]
        b_row = b_t[SLOT_F + h:SLOT_F + h + 1, :]
        ig_row = gate_t[SLOT_I + h:SLOT_I + h + 1, :]
        m_prev = m_in[h]
        ct_prev = ct_in[h]

        dmat = jnp.where(tri, b_col - b_row + ig_row, -jnp.inf)
        inter = b_col + m_prev
        m_t = jnp.maximum(jnp.max(dmat, axis=1, keepdims=True), inter)
        p = _dot(q, k_t) * jnp.exp(dmat - m_t)
        w_inter = jnp.exp(inter - m_t)
        num_den = _dot(p.astype(BF16), v_ext) + w_inter * _dot(q, ct_prev.astype(BF16))
        den = num_den[:, dh:dh + 1]
        denom = jnp.maximum(jnp.abs(den), jnp.exp(-m_t))
        out = num_den[:, :dh] / denom * _sigmoid(o_ref[:, h * dh:(h + 1) * dh].astype(F32))
        outs.append(out.astype(y_ref.dtype))

        b_end = b_col[n - 1:n, :]
        a_end = b_end - b_col + ig_col
        m_loc = jnp.max(a_end, axis=0, keepdims=True)
        w_end = jnp.exp(a_end - m_loc)
        m_new = jnp.maximum(b_end + m_prev, m_loc)
        s_prev = jnp.exp(b_end + m_prev - m_new)
        s_loc = jnp.exp(m_loc - m_new)
        upd = _dot(k_t, (w_end * v_ext.astype(F32)).astype(BF16))
        ct_out.append(s_prev * ct_prev + s_loc * upd)
        m_out.append(jnp.broadcast_to(m_new, m_scr.shape[1:]))

    y_ref[...] = jnp.concatenate(outs, axis=1)
    for h in range(ML_HEADS):
        ct_scr[h] = ct_out[h]
        m_scr[h] = m_out[h]


def _seq_specs(n, bsz):
    row = lambda w: pl.BlockSpec((bsz, n, w), lambda c: (0, c, 0))
    tail = lambda w: pl.BlockSpec((bsz, TAIL, w), lambda c: (0, jnp.maximum(c * (n // TAIL) - 1, 0), 0))
    return row, tail


def _by_seq(a, bsz):
    return a.reshape(bsz, a.shape[0] // bsz, a.shape[1])


def _mlstm_call_parts(qk, v, o, small, conv_w, conv_b, gbias, n, bsz):
    row, tail = _seq_specs(n, bsz)
    qk, v, o, small = (_by_seq(a, bsz) for a in (qk, v, o, small))
    in_specs = [row(2 * ML_WIDTH), tail(2 * ML_WIDTH), row(ML_WIDTH), row(ML_WIDTH), row(LANES),
                _const_spec((CONV_K, 2 * ML_WIDTH)), _const_spec((1, 2 * ML_WIDTH)), _const_spec((1, LANES))]
    args = (qk, qk, v, o, small, conv_w, conv_b.reshape(1, -1), gbias)
    scratch = [pltpu.VMEM((bsz, ML_HEADS, ML_HEAD_DIM, 2 * ML_HEAD_DIM), F32),
               pltpu.VMEM((bsz, ML_HEADS, 1, LANES), F32)]
    return in_specs, args, 5, row(ML_WIDTH), jax.ShapeDtypeStruct(qk.shape[:2] + (ML_WIDTH,), BF16), scratch


def _ssd_kernel(z_ref, xbc_ref, tail_ref, small_ref, cw_ref, cb_ref, gbias_ref, alog_ref, dskip_ref, normw_ref, expand_ref,
                y_ref, state_scr):
    first = pl.program_id(0) == 0
    n = z_ref.shape[0]
    gw = M2_GROUP_WIDTH
    hd = M2_HEAD_DIM

    @pl.when(first)
    def _():
        state_scr[...] = jnp.zeros_like(state_scr)

    state_in = [state_scr[g] for g in range(M2_GROUPS)]
    xbc = _silu(_causal_conv(xbc_ref, tail_ref, cw_ref, cb_ref, first))
    xs = xbc[:, :M2_WIDTH]
    lane = lax.broadcasted_iota(jnp.int32, (1, LANES), 1)
    dt_lanes = (lane >= SLOT_DT) & (lane < SLOT_DT + M2_HEADS)
    dt = jnp.where(dt_lanes, jax.nn.softplus(small_ref[...] + gbias_ref[...]), 0.0)
    a_row = jnp.where(dt_lanes, -jnp.exp(alog_ref[...]), 0.0)
    tri = _lower_tri(n)
    cum = _dot3_left(tri.astype(BF16), dt * a_row)
    cum_t = cum.T
    cum_end = cum[n - 1:n, :]
    expand = expand_ref[...]
    dt_e = _dot(dt.astype(BF16), expand)
    decay_in = _dot(jnp.exp(cum).astype(BF16), expand)
    decay_out = _dot((dt * jnp.exp(cum_end - cum)).astype(BF16), expand)
    decay_chunk = _dot3_right(jnp.exp(cum[n - SUBLANES:n, :]), expand)[SUBLANES - 1:SUBLANES, :]
    xdt_b = (xs * dt_e).astype(BF16)
    x_in = (xs * decay_out).astype(BF16)
    left_half = lax.broadcasted_iota(jnp.int32, (n, 2 * hd), 1) < hd

    y_out, state_out = [], []
    for g in range(M2_GROUPS):
        b_g = xbc[:, M2_WIDTH + g * M2_STATE:M2_WIDTH + (g + 1) * M2_STATE]
        c_g = xbc[:, M2_WIDTH + (M2_GROUPS + g) * M2_STATE:M2_WIDTH + (M2_GROUPS + g + 1) * M2_STATE].astype(BF16)
        b_t = b_g.T.astype(BF16)
        cb = _dot(c_g, b_t)
        y_diag = []
        for pair in range(M2_HPG // 2):
            head = g * M2_HPG + 2 * pair
            ws = []
            for hh in (head, head + 1):
                cum_col = cum[:, SLOT_DT + hh:SLOT_DT + hh + 1]
                cum_row = cum_t[SLOT_DT + hh:SLOT_DT + hh + 1, :]
                ws.append((cb * jnp.exp(jnp.where(tri, cum_col - cum_row, -jnp.inf))).astype(BF16))
            xp = xdt_b[:, head * hd:(head + 2) * hd]
            zero = jnp.zeros_like(xp)
            rhs = jnp.concatenate([jnp.where(left_half, xp, zero), jnp.where(left_half, zero, xp)], axis=0)
            y_diag.append(_dot(jnp.concatenate(ws, axis=1), rhs))
        cols = slice(g * gw, (g + 1) * gw)
        st = state_in[g]
        y_off = _dot(c_g, st.astype(BF16)) * decay_in[:, cols]
        y = jnp.concatenate(y_diag, axis=1) + y_off + dskip_ref[:, cols] * xs[:, cols]
        y = y * _silu(z_ref[:, cols].astype(F32))
        y = y * lax.rsqrt(jnp.mean(y * y, axis=-1, keepdims=True) + EPS)
        y_out.append((y * normw_ref[:, cols]).astype(y_ref.dtype))
        state_out.append(st * decay_chunk[:, cols] + _dot(b_t, x_in[:, cols]))

    y_ref[...] = jnp.concatenate(y_out, axis=1)
    for g in range(M2_GROUPS):
        state_scr[g] = state_out[g]


def _ssd_call_parts(z, xbc, small, conv_w, conv_b, gbias, alog_row, dskip_row, norm_w, n, bsz):
    row, tail = _seq_specs(n, bsz)
    z, xbc, small = (_by_seq(a, bsz) for a in (z, xbc, small))
    expand = np.zeros((LANES, M2_WIDTH), np.float32)
    for h in range(M2_HEADS):
        expand[SLOT_DT + h, h * M2_HEAD_DIM:(h + 1) * M2_HEAD_DIM] = 1.0
    in_specs = [row(M2_WIDTH), row(M2_CONV_DIM), tail(M2_CONV_DIM), row(LANES),
                _const_spec((CONV_K, M2_CONV_DIM)), _const_spec((1, M2_CONV_DIM)), _const_spec((1, LANES)), _const_spec((1, LANES)),
                _const_spec((1, M2_WIDTH)), _const_spec((1, M2_WIDTH)), _const_spec((LANES, M2_WIDTH))]
    args = (z, xbc, xbc, small, conv_w, conv_b.reshape(1, -1), gbias, alog_row, dskip_row, norm_w.reshape(1, -1),
            jnp.asarray(expand, BF16))
    scratch = [pltpu.VMEM((bsz, M2_GROUPS, M2_STATE, M2_GROUP_WIDTH), F32)]
    return in_specs, args, 4, row(M2_WIDTH), jax.ShapeDtypeStruct(z.shape[:2] + (M2_WIDTH,), BF16), scratch


def _mixers(ml_inputs, ssd_inputs, bsz):
    t = ml_inputs[0].shape[0]
    seq = t // bsz
    n = min(SEQ_CHUNK, seq)
    ml_specs, ml_args, ml_nseq, ml_out_spec, ml_out_shape, ml_scratch = _mlstm_call_parts(*ml_inputs, n, bsz)
    ssd_specs, ssd_args, ssd_nseq, ssd_out_spec, ssd_out_shape, ssd_scratch = _ssd_call_parts(*ssd_inputs, n, bsz)
    n_ml, n_ssd = len(ml_args), len(ssd_args)

    def body(*refs):
        ins, outs, scr = refs[:n_ml + n_ssd], refs[n_ml + n_ssd:n_ml + n_ssd + 2], refs[n_ml + n_ssd + 2:]
        for b in range(bsz):
            per_seq = lambda rs, k: [r.at[b] if i < k else r for i, r in enumerate(rs)]
            _mlstm_kernel(*per_seq(ins[:n_ml], ml_nseq), outs[0].at[b], *[s.at[b] for s in scr[:len(ml_scratch)]])
            _ssd_kernel(*per_seq(ins[n_ml:], ssd_nseq), outs[1].at[b], *[s.at[b] for s in scr[len(ml_scratch):]])

    yb, yc = pl.pallas_call(
        body,
        grid=(seq // n,),
        in_specs=ml_specs + ssd_specs,
        out_specs=[ml_out_spec, ssd_out_spec],
        out_shape=[ml_out_shape, ssd_out_shape],
        scratch_shapes=ml_scratch + ssd_scratch,
        compiler_params=_cparams("arbitrary"),
        name="mixers",
    )(*ml_args, *ssd_args)
    return yb.reshape(t, ML_WIDTH), yc.reshape(t, M2_WIDTH)


def _merge_kernel(x_ref, yg_ref, yb_ref, yc_ref, g_ref, wglu_ref, wa_ref, wb_ref, wc_ref, wout_ref, nmp_ref, out_ref, ys_scr):
    d = D_MODEL
    gate = lambda i: _sigmoid(g_ref[:, i * d:(i + 1) * d].astype(F32))
    merged_bc = gate(1) * _dot(yb_ref[...], wb_ref[...]) + gate(2) * _dot(yc_ref[...], wc_ref[...])
    nchunks = yg_ref.shape[1]
    for jb in range(S5_GROUPS // GROUPS_PER_VREG):
        for half in range(S5_CHUNK // GROUPS_PER_VREG):
            v = [yg_ref[jb * GROUPS_PER_VREG + g, :, half * LANES:(half + 1) * LANES].astype(F32) for g in range(GROUPS_PER_VREG)]
            v = _block_transpose(v)
            for t in range(GROUPS_PER_VREG):
                ys_scr[jb, pl.ds(half * GROUPS_PER_VREG + t, nchunks, stride=S5_CHUNK), :] = v[t]
    ys5 = jnp.concatenate([ys_scr[jb] for jb in range(S5_WIDTH // LANES)], axis=1)
    ya = ys5 * _sigmoid(_dot(ys5.astype(BF16), wglu_ref[...]))
    merged = gate(0) * _dot(ya.astype(BF16), wa_ref[...]) + merged_bc
    mix = _dot(merged.astype(BF16), wout_ref[...])
    out_ref[...] = x_ref[...] + _rms(mix, nmp_ref[...])


def _merge(x2, yg, yb, yc, gates, weights, norm_post, layer):
    t = x2.shape[0]
    tm = min(TOKEN_BLOCK, t)
    row = lambda w: pl.BlockSpec((tm, w), lambda i: (i, 0))
    return pl.pallas_call(
        _merge_kernel,
        grid=(t // tm,),
        in_specs=[row(D_MODEL), pl.BlockSpec((S5_GROUPS, tm // S5_CHUNK, S5_K), lambda i: (0, i, 0)),
                  row(ML_WIDTH), row(M2_WIDTH), row(N_BRANCH * D_MODEL)]
        + [_layer_spec(w, layer) for w in weights] + [_const_spec((1, D_MODEL))],
        out_specs=row(D_MODEL),
        out_shape=jax.ShapeDtypeStruct((t, D_MODEL), F32),
        scratch_shapes=[pltpu.VMEM((S5_WIDTH // LANES, tm, LANES), F32)],
        compiler_params=_cparams("parallel"),
        name="merge",
    )(x2, yg, yb, yc, gates, *weights, norm_post.reshape(1, D_MODEL))


def _ffn_kernel(x_ref, npre_ref, npost_ref, wg_ref, wu_ref, wd_ref, out_ref):
    x = x_ref[...]
    hb = _rms(x, npre_ref[...]).astype(BF16)
    act = _silu(_dot(hb, wg_ref[...])) * _dot(hb, wu_ref[...])
    ffn = _dot(act.astype(BF16), wd_ref[...])
    out_ref[...] = x + _rms(ffn, npost_ref[...])


def _ffn(x2, weights, norm_pre, norm_post, layer):
    t = x2.shape[0]
    tm = min(TOKEN_BLOCK, t)
    row = pl.BlockSpec((tm, D_MODEL), lambda i: (i, 0))
    return pl.pallas_call(
        _ffn_kernel,
        grid=(t // tm,),
        in_specs=[row, _const_spec((1, D_MODEL)), _const_spec((1, D_MODEL))] + [_layer_spec(w, layer) for w in weights],
        out_specs=row,
        out_shape=jax.ShapeDtypeStruct((t, D_MODEL), F32),
        compiler_params=_cparams("parallel"),
        name="ffn",
    )(x2, norm_pre.reshape(1, D_MODEL), norm_post.reshape(1, D_MODEL), *weights)


def _gate_bias_row(bias_i, bias_f, dt_bias):
    pad = jnp.zeros((LANES - 2 * ML_HEADS - M2_HEADS,), F32)
    return jnp.concatenate([bias_i, bias_f, dt_bias, pad]).reshape(1, LANES)


def _head_lane_row(v):
    pad_l = jnp.zeros((SLOT_DT,), F32)
    pad_r = jnp.zeros((LANES - SLOT_DT - M2_HEADS,), F32)
    return jnp.concatenate([pad_l, v, pad_r]).reshape(1, LANES)


def _layer(x2, bsz, p, big, layer):
    ug, qk, v, o, z, xbc, gates, small = _in_proj(x2, p["norm_mix_pre"], big["w_in"], layer)
    mats = _s5_prep(p["s5_a_re"], p["s5_a_im"], p["s5_b_re"], p["s5_b_im"], p["s5_c_re"], p["s5_c_im"], p["s5_d"], p["s5_log_dt"])
    yg = _s5_mixer(ug, bsz, mats)
    gbias = _gate_bias_row(p["ml_bias_i"], p["ml_bias_f"], p["m2_dt_bias"])
    yb, yc = _mixers(
        (qk, v, o, small, p["ml_conv_w"], p["ml_conv_b"], gbias),
        (z, xbc, small, p["m2_conv_w"], p["m2_conv_b"], gbias, _head_lane_row(p["m2_a_log"]),
         jnp.repeat(p["m2_d"], M2_HEAD_DIM).reshape(1, M2_WIDTH), p["m2_norm_w"]),
        bsz)
    x1 = _merge(x2, yg, yb, yc, gates, big["merge"], p["norm_mix_post"], layer)
    return _ffn(x1, big["ffn"], p["norm_ffn_pre"], p["norm_ffn_post"], layer)


_PARAM_NAMES = ("norm_mix_pre", "norm_mix_post", "w_in", "s5_a_re", "s5_a_im", "s5_b_re", "s5_b_im", "s5_c_re", "s5_c_im", "s5_d",
                "s5_log_dt", "s5_w_glu", "ml_conv_w", "ml_conv_b", "ml_bias_i", "ml_bias_f", "m2_conv_w", "m2_conv_b", "m2_dt_bias",
                "m2_a_log", "m2_d", "m2_norm_w", "w_br_a", "w_br_b", "w_br_c", "w_out", "norm_ffn_pre", "norm_ffn_post",
                "w_ffn_gate", "w_ffn_up", "w_ffn_down")


def kernel(x, norm_mix_pre, norm_mix_post, w_in, s5_a_re, s5_a_im, s5_b_re, s5_b_im, s5_c_re, s5_c_im, s5_d, s5_log_dt, s5_w_glu, ml_conv_w, ml_conv_b, ml_bias_i, ml_bias_f, m2_conv_w, m2_conv_b, m2_dt_bias, m2_a_log, m2_d, m2_norm_w, w_br_a, w_br_b, w_br_c, w_out, norm_ffn_pre, norm_ffn_post, w_ffn_gate, w_ffn_up, w_ffn_down):
    stacked = (norm_mix_pre, norm_mix_post, w_in, s5_a_re, s5_a_im, s5_b_re, s5_b_im, s5_c_re, s5_c_im, s5_d, s5_log_dt, s5_w_glu,
               ml_conv_w, ml_conv_b, ml_bias_i, ml_bias_f, m2_conv_w, m2_conv_b, m2_dt_bias, m2_a_log, m2_d, m2_norm_w,
               w_br_a, w_br_b, w_br_c, w_out, norm_ffn_pre, norm_ffn_post, w_ffn_gate, w_ffn_up, w_ffn_down)
    bsz, seq, d = x.shape
    x2 = x.reshape(bsz * seq, d)
    big = {
        "w_in": _split_w_in(w_in),
        "merge": [w.astype(BF16) for w in (s5_w_glu, w_br_a, w_br_b, w_br_c, w_out)],
        "ffn": [w.astype(BF16) for w in (w_ffn_gate, w_ffn_up, w_ffn_down)],
    }
    for layer in range(norm_mix_pre.shape[0]):
        x2 = _layer(x2, bsz, {name: arr[layer] for name, arr in zip(_PARAM_NAMES, stacked)}, big, layer)
    return x2.reshape(bsz, seq, d)
```
